```python
import jax
import jax.numpy as jnp
from jax import lax

D_MODEL = 1024
BATCH = 4
SEQ = 4096
DEPTH = 4
DEC_BATCH = 32
DEC_SEQ = 1
PAST_LEN = 8192
PAGE_SIZE = 128

N_MIXERS = 4
DN_ALPHA = (2.0 * DEPTH) ** 0.25
DN_BETA = (8.0 * DEPTH) ** -0.25
LN_EPS = 1e-5
CONV_W = 3
Q_BLOCK = 128

SC_WIDTH = D_MODEL

SB_HEADS = 16
SB_HEAD_DIM = D_MODEL // SB_HEADS
SB_BIAS_LO = -9.0
SB_BIAS_HI = -3.0

GLA_HEADS = 4
GLA_DK = D_MODEL // 2 // GLA_HEADS
GLA_DV = D_MODEL // GLA_HEADS
GLA_RANK = 16
GLA_TAU = 16.0
GLA_CHUNK = 64
GLA_QK = GLA_HEADS * GLA_DK
GLA_V = GLA_HEADS * GLA_DV
GLA_SPLITS = (GLA_QK, 2 * GLA_QK, 2 * GLA_QK + GLA_V, 2 * GLA_QK + 2 * GLA_V)
GLA_IN = 2 * GLA_QK + 2 * GLA_V + GLA_RANK

DSW_GROUPS = ((128, 1), (512, 4), (2048, 16))
DSW_N_GROUPS = len(DSW_GROUPS)
DSW_HEADS = 8
DSW_HEAD_DIM = 64
DSW_INNER = DSW_HEADS * DSW_HEAD_DIM

D_FF = 2816

kernel_name = 'hybrid_shortconv_stickbreak_gla_dilated_decode_step'


def layer_norm(x, g, b):
    xf = x.astype(jnp.float32)
    mu = jnp.mean(xf, axis=-1, keepdims=True)
    var = jnp.mean(jnp.square(xf - mu), axis=-1, keepdims=True)
    return ((xf - mu) * lax.rsqrt(var + LN_EPS) * g + b).astype(x.dtype)


def causal_dwconv(z, prev, w):
    n = z.shape[1]
    zp = jnp.concatenate([prev.astype(z.dtype), z], axis=1)
    y = w[0] * zp[:, :n] + w[1] * zp[:, 1:n + 1] + w[2] * zp[:, 2:n + 2]
    return y, zp[:, n:]


def short_conv_mixer(x, prev, w_in, w_conv, w_out):
    b_gate, c_gate, v = jnp.split(x @ w_in, 3, axis=-1)
    z, new_prev = causal_dwconv(c_gate * v, prev, w_conv)
    return (b_gate * z) @ w_out, new_prev


def sb_qkv(x, w_qkv):
    bsz, n = x.shape[:2]
    qkv = (x @ w_qkv).reshape(bsz, n, 3, SB_HEADS, SB_HEAD_DIM)
    return qkv[:, :, 0], qkv[:, :, 1], qkv[:, :, 2]


def stick_breaking_attend(q, k, v, q_pos, k_pos, bias):
    z = jnp.einsum('bqhd,bkhd->bhqk', q, k).astype(jnp.float32) * (SB_HEAD_DIM ** -0.5)
    z = z + bias.astype(jnp.float32)[:, None, None]
    causal = k_pos[None, :] < q_pos[:, None]
    log_rest = jnp.where(causal, jax.nn.log_sigmoid(-z), 0.0)
    between = lax.cumsum(log_rest, axis=3, reverse=True) - log_rest
    weight = jnp.where(causal, jnp.exp(jax.nn.log_sigmoid(z) + between), 0.0)
    return jnp.einsum('bhqk,bkhd->bqhd', weight.astype(v.dtype), v)


def sb_mixer_prompt(x, w_qkv, w_out, bias):
    bsz, n = x.shape[:2]
    q, k, v = sb_qkv(x, w_qkv)
    k_pos = jnp.arange(n)

    def block(i):
        start = i * Q_BLOCK
        q_blk = lax.dynamic_slice_in_dim(q, start, Q_BLOCK, axis=1)
        return stick_breaking_attend(q_blk, k, v, start + jnp.arange(Q_BLOCK), k_pos, bias)

    o = lax.map(block, jnp.arange(n // Q_BLOCK))
    o = jnp.moveaxis(o, 0, 1).reshape(bsz, n, SB_HEADS * SB_HEAD_DIM)
    return o @ w_out, k, v


def sb_mixer_sample(x, cache_k, cache_v, page_table, w_qkv, w_out, bias):
    bsz, n = x.shape[:2]
    q, k, v = sb_qkv(x, w_qkv)
    past = page_table.shape[1] * PAGE_SIZE
    k_past = cache_k[page_table].reshape(bsz, past, SB_HEADS, SB_HEAD_DIM).astype(k.dtype)
    v_past = cache_v[page_table].reshape(bsz, past, SB_HEADS, SB_HEAD_DIM).astype(v.dtype)
    k_all = jnp.concatenate([k_past, k], axis=1)
    v_all = jnp.concatenate([v_past, v], axis=1)
    o = stick_breaking_attend(q, k_all, v_all, past + jnp.arange(n), jnp.arange(past + n), bias)
    return o.reshape(bsz, n, SB_HEADS * SB_HEAD_DIM) @ w_out, k, v


def gla_chunk(s0, q, k, v, log_a):
    n = q.shape[1]
    b = jnp.cumsum(log_a, axis=1)
    causal = jnp.tril(jnp.ones((n, n), dtype=bool))[None, :, :, None, None]
    rel = jnp.where(causal, b[:, :, None] - b[:, None, :], -jnp.inf)
    scores = jnp.einsum('bthd,bshd,btshd->bhts', q, k, jnp.exp(rel))
    o = jnp.einsum('bhts,bshv->bthv', scores, v) + jnp.einsum('bthd,bhdv->bthv', q * jnp.exp(b), s0)
    b_last = b[:, -1]
    s_new = jnp.exp(b_last)[..., None] * s0 + jnp.einsum('bshd,bshv->bhdv', k * jnp.exp(b_last[:, None] - b), v)
    return o, s_new


def gla_mixer(x, s0, w_in, w_gate_up, b_gate, g_norm, w_out):
    bsz, n = x.shape[:2]
    f32 = jnp.float32
    q, k, v, r, g_low = jnp.split(x @ w_in, GLA_SPLITS, axis=-1)
    log_a = jax.nn.log_sigmoid((g_low @ w_gate_up + b_gate).astype(f32)) / GLA_TAU

    def heads(t, d):
        return t.reshape(bsz, n, GLA_HEADS, d).astype(f32)

    q = heads(q, GLA_DK) * (GLA_DK ** -0.5)
    k = heads(k, GLA_DK)
    v = heads(v, GLA_DV)
    log_a = heads(log_a, GLA_DK)
    chunk = GLA_CHUNK if n % GLA_CHUNK == 0 else n
    n_chunks = n // chunk

    def to_chunks(t):
        return jnp.moveaxis(t.reshape(bsz, n_chunks, chunk, GLA_HEADS, t.shape[-1]), 1, 0)

    def step(s, xs):
        o_c, s_next = gla_chunk(s, *xs)
        return s_next, o_c

    s_final, o = lax.scan(step, s0.astype(f32), (to_chunks(q), to_chunks(k), to_chunks(v), to_chunks(log_a)))
    o = jnp.moveaxis(o, 0, 1).reshape(bsz, n, GLA_HEADS, GLA_DV)
    o = o * lax.rsqrt(jnp.mean(o * o, axis=-1, keepdims=True) + LN_EPS) * g_norm
    o = o.reshape(bsz, n, GLA_V).astype(x.dtype) * jax.nn.silu(r)
    return o @ w_out, s_final.astype(s0.dtype)


def alibi_slopes():
    n_heads = DSW_N_GROUPS * DSW_HEADS
    m = 2.0 ** (-8.0 * jnp.arange(1, n_heads + 1, dtype=jnp.float32) / n_heads)
    return m.reshape(DSW_N_GROUPS, DSW_HEADS)


def dilated_attend(q, k_ctx, v_ctx, q_idx, slopes, window, dilation):
    dist = dilation * jnp.arange(window // dilation + 1)
    idx = q_idx[:, None] - dist[None, :]
    valid = idx >= 0
    idx = jnp.maximum(idx, 0)
    k_g = k_ctx[:, idx]
    v_g = v_ctx[:, idx]
    s = jnp.einsum('bqhd,bqnhd->bhqn', q, k_g).astype(jnp.float32) * (DSW_HEAD_DIM ** -0.5)
    s = s - slopes[:, None, None] * dist.astype(jnp.float32)
    s = jnp.where(valid, s, -jnp.inf)
    m = jnp.max(s, axis=-1, keepdims=True)
    p = jnp.exp(s - m)
    l = jnp.sum(p, axis=-1, keepdims=True)
    o = jnp.einsum('bhqn,bqnhd->bqhd', (p / l).astype(v_ctx.dtype), v_g)
    return o, (m + jnp.log(l))[..., 0]


def dilated_mix(qs, ks, vs, q_idxs):
    slopes = alibi_slopes()
    outs, lses = [], []
    for g, (window, dilation) in enumerate(DSW_GROUPS):
        o, lse = dilated_attend(qs[g], ks[g], vs[g], q_idxs[g], slopes[g], window, dilation)
        outs.append(o.astype(jnp.float32))
        lses.append(lse)
    w = jax.nn.softmax(jnp.stack(lses), axis=0)
    w = jnp.transpose(w, (0, 1, 3, 2))[..., None]
    return jnp.sum(jnp.stack(outs) * w, axis=0)


def dsw_qkv(x, w_qkv):
    bsz, n = x.shape[:2]
    return (x @ w_qkv).reshape(bsz, n, DSW_N_GROUPS, 3, DSW_HEADS, DSW_HEAD_DIM)


def dsw_mixer_prompt(x, w_qkv, w_out):
    bsz, n = x.shape[:2]
    qkv = dsw_qkv(x, w_qkv)
    ks = [qkv[:, :, g, 1] for g in range(DSW_N_GROUPS)]
    vs = [qkv[:, :, g, 2] for g in range(DSW_N_GROUPS)]

    def block(i):
        start = i * Q_BLOCK
        qs = [lax.dynamic_slice_in_dim(qkv[:, :, g, 0], start, Q_BLOCK, axis=1) for g in range(DSW_N_GROUPS)]
        q_idx = start + jnp.arange(Q_BLOCK)
        return dilated_mix(qs, ks, vs, [q_idx] * DSW_N_GROUPS)

    o = lax.map(block, jnp.arange(n // Q_BLOCK))
    o = jnp.moveaxis(o, 0, 1).reshape(bsz, n, DSW_INNER).astype(x.dtype)
    new_kv = [qkv[:, n - min(w, n):, g, 1:] for g, (w, _) in enumerate(DSW_GROUPS)]
    return o @ w_out, new_kv


def dsw_mixer_sample(x, bufs, w_qkv, w_out):
    bsz, n = x.shape[:2]
    qkv = dsw_qkv(x, w_qkv)
    ctx = [jnp.concatenate([bufs[g].astype(qkv.dtype), qkv[:, :, g, 1:]], axis=1) for g in range(DSW_N_GROUPS)]
    qs = [qkv[:, :, g, 0] for g in range(DSW_N_GROUPS)]
    q_idxs = [bufs[g].shape[1] + jnp.arange(n) for g in range(DSW_N_GROUPS)]
    o = dilated_mix(qs, [c[:, :, 0] for c in ctx], [c[:, :, 1] for c in ctx], q_idxs)
    new_bufs = [c[:, n:] for c in ctx]
    return o.reshape(bsz, n, DSW_INNER).astype(x.dtype) @ w_out, new_bufs


def conv_ffn(x, prev, w_up, w_conv, w_down):
    gate, up = jnp.split(x @ w_up, 2, axis=-1)
    gate, new_prev = causal_dwconv(gate, prev, w_conv)
    return (jax.nn.silu(gate) * up) @ w_down, new_prev


def setup_inputs(seed: int = 0) -> dict:
    key = jax.random.key(seed)
    ks = jax.random.split(key, 32)
    f32 = jnp.float32

    def nrm(k, shape, scale=1.0):
        return jax.random.normal(k, shape, f32) * scale

    n_pages = PAST_LEN // PAGE_SIZE
    n_used = DEC_BATCH * n_pages
    n_pool = n_used + max(1, n_used // 4)
    page_table = jax.random.permutation(ks[0], n_pool)[:n_used].reshape(DEC_BATCH, n_pages).astype(jnp.int32)
    dsw_bufs = [nrm(ks[1 + g], (DEC_BATCH, min(w, PAST_LEN), 2, DSW_HEADS, DSW_HEAD_DIM))
                for g, (w, _) in enumerate(DSW_GROUPS)]
    return {
        'x_prompt': nrm(ks[4], (BATCH, SEQ, D_MODEL)),
        'x_sample': nrm(ks[5], (DEC_BATCH, DEC_SEQ, D_MODEL)),
        'cache_sc_conv': nrm(ks[6], (DEC_BATCH, CONV_W - 1, SC_WIDTH)),
        'cache_sb_k': nrm(ks[7], (n_pool, PAGE_SIZE, SB_HEADS, SB_HEAD_DIM)),
        'cache_sb_v': nrm(ks[8], (n_pool, PAGE_SIZE, SB_HEADS, SB_HEAD_DIM)),
        'state_gla': nrm(ks[9], (DEC_BATCH, GLA_HEADS, GLA_DK, GLA_DV)),
        'cache_dsw_kv0': dsw_bufs[0],
        'cache_dsw_kv1': dsw_bufs[1],
        'cache_dsw_kv2': dsw_bufs[2],
        'state_ffn_conv': nrm(ks[10], (DEPTH, DEC_BATCH, CONV_W - 1, D_FF)),
        'page_table': page_table,
        'ln_g': 1.0 + nrm(ks[11], (DEPTH, 2, D_MODEL), 0.02),
        'ln_b': nrm(ks[12], (DEPTH, 2, D_MODEL), 0.02),
        'w_sc_in': nrm(ks[13], (D_MODEL, 3 * SC_WIDTH), D_MODEL ** -0.5),
        'w_sc_conv': nrm(ks[14], (CONV_W, SC_WIDTH), CONV_W ** -0.5),
        'w_sc_out': nrm(ks[15], (SC_WIDTH, D_MODEL), DN_BETA * SC_WIDTH ** -0.5),
        'w_sb_qkv': nrm(ks[16], (D_MODEL, 3 * SB_HEADS * SB_HEAD_DIM), D_MODEL ** -0.5),
        'w_sb_out': nrm(ks[17], (SB_HEADS * SB_HEAD_DIM, D_MODEL), DN_BETA * (SB_HEADS * SB_HEAD_DIM) ** -0.5),
        'b_sb': jnp.linspace(SB_BIAS_LO, SB_BIAS_HI, SB_HEADS, dtype=f32) + nrm(ks[28], (SB_HEADS,), 0.1),
        'w_gla_in': nrm(ks[18], (D_MODEL, GLA_IN), D_MODEL ** -0.5),
        'w_gla_gate_up': nrm(ks[19], (GLA_RANK, GLA_QK), GLA_RANK ** -0.5),
        'b_gla_gate': nrm(ks[20], (GLA_QK,), 0.1),
        'g_gla_norm': 1.0 + nrm(ks[21], (GLA_DV,), 0.02),
        'w_gla_out': nrm(ks[22], (GLA_V, D_MODEL), DN_BETA * GLA_V ** -0.5),
        'w_dsw_qkv': nrm(ks[23], (D_MODEL, DSW_N_GROUPS * 3 * DSW_INNER), D_MODEL ** -0.5),
        'w_dsw_out': nrm(ks[24], (DSW_INNER, D_MODEL), DN_BETA * DSW_INNER ** -0.5),
        'w_ffn_up': nrm(ks[25], (DEPTH, D_MODEL, 2 * D_FF), D_MODEL ** -0.5),
        'w_ffn_conv': nrm(ks[26], (DEPTH, CONV_W, D_FF), CONV_W ** -0.5),
        'w_ffn_down': nrm(ks[27], (DEPTH, D_FF, D_MODEL), DN_BETA * D_FF ** -0.5),
    }


def reference(x_prompt, x_sample, cache_sc_conv, cache_sb_k, cache_sb_v, state_gla,
              cache_dsw_kv0, cache_dsw_kv1, cache_dsw_kv2, state_ffn_conv, page_table,
              ln_g, ln_b, w_sc_in, w_sc_conv, w_sc_out, w_sb_qkv, w_sb_out, b_sb,
              w_gla_in, w_gla_gate_up, b_gla_gate, g_gla_norm, w_gla_out,
              w_dsw_qkv, w_dsw_out, w_ffn_up, w_ffn_conv, w_ffn_down):
    xp, xs = x_prompt, x_sample
    bp = xp.shape[0]
    ffn_p, ffn_s = [], []
    for layer in range(DEPTH):
        kind = layer % N_MIXERS
        if kind == 0:
            mp, sc_p = short_conv_mixer(xp, jnp.zeros((bp, CONV_W - 1, SC_WIDTH), xp.dtype), w_sc_in, w_sc_conv, w_sc_out)
            ms, sc_s = short_conv_mixer(xs, cache_sc_conv, w_sc_in, w_sc_conv, w_sc_out)
        elif kind == 1:
            mp, sb_k_p, sb_v_p = sb_mixer_prompt(xp, w_sb_qkv, w_sb_out, b_sb)
            ms, sb_k_s, sb_v_s = sb_mixer_sample(xs, cache_sb_k, cache_sb_v, page_table, w_sb_qkv, w_sb_out, b_sb)
        elif kind == 2:
            s0 = jnp.zeros((bp, GLA_HEADS, GLA_DK, GLA_DV), state_gla.dtype)
            mp, gla_p = gla_mixer(xp, s0, w_gla_in, w_gla_gate_up, b_gla_gate, g_gla_norm, w_gla_out)
            ms, gla_s = gla_mixer(xs, state_gla, w_gla_in, w_gla_gate_up, b_gla_gate, g_gla_norm, w_gla_out)
        else:
            mp, dsw_p = dsw_mixer_prompt(xp, w_dsw_qkv, w_dsw_out)
            ms, dsw_s = dsw_mixer_sample(xs, [cache_dsw_kv0, cache_dsw_kv1, cache_dsw_kv2], w_dsw_qkv, w_dsw_out)
        xp = layer_norm(DN_ALPHA * xp + mp, ln_g[layer, 0], ln_b[layer, 0])
        xs = layer_norm(DN_ALPHA * xs + ms, ln_g[layer, 0], ln_b[layer, 0])
        fp, fprev_p = conv_ffn(xp, jnp.zeros((bp, CONV_W - 1, D_FF), xp.dtype), w_ffn_up[layer], w_ffn_conv[layer], w_ffn_down[layer])
        fs, fprev_s = conv_ffn(xs, state_ffn_conv[layer], w_ffn_up[layer], w_ffn_conv[layer], w_ffn_down[layer])
        ffn_p.append(fprev_p)
        ffn_s.append(fprev_s)
        xp = layer_norm(DN_ALPHA * xp + fp, ln_g[layer, 1], ln_b[layer, 1])
        xs = layer_norm(DN_ALPHA * xs + fs, ln_g[layer, 1], ln_b[layer, 1])
    return (xp, xs, sc_p, sc_s, sb_k_p, sb_v_p, sb_k_s, sb_v_s, gla_p, gla_s,
            dsw_p[0], dsw_p[1], dsw_p[2], dsw_s[0], dsw_s[1], dsw_s[2],
            jnp.stack(ffn_p), jnp.stack(ffn_s))
```

```python
import functools

import jax
import jax.numpy as jnp
from jax import lax
from jax.experimental import pallas as pl
from jax.experimental.pallas import tpu as pltpu

F32 = jnp.float32
BF16 = jnp.bfloat16

D_MODEL = 1024
DEPTH = 4
DN_ALPHA = (2.0 * DEPTH) ** 0.25
LN_EPS = 1e-5
PAGE_SIZE = 128

SB_HEADS = 16
SB_HEAD_DIM = 64

GLA_HEADS = 4
GLA_DK = 128
GLA_DV = 256
GLA_RANK = 16
GLA_TAU = 16.0
GLA_QK = GLA_HEADS * GLA_DK
GLA_V = GLA_HEADS * GLA_DV
GLA_CHUNK = 128

DSW_GROUPS = ((128, 1), (512, 4), (2048, 16))
DSW_HEADS = 8
DSW_HEAD_DIM = 64
DSW_INNER = DSW_HEADS * DSW_HEAD_DIM
DSW_TILE = 2048
DSW_QB = 128

D_FF = 2816
MXU_COLS = 256
V7X_VMEM_BYTES = 64 * 1024 * 1024
VMEM_CAP = V7X_VMEM_BYTES - 6 * 1024 * 1024


def _vmem_limit(block_bytes):
    return int(min(VMEM_CAP, block_bytes * 5 // 4 + (8 << 20)))


def _params(n_grid, block_bytes):
    return pltpu.CompilerParams(dimension_semantics=("arbitrary",) * n_grid,
                                vmem_limit_bytes=_vmem_limit(block_bytes))


def _resident(shape):
    nd = len(shape)
    return pl.BlockSpec(shape, lambda *_: (0,) * nd, pipeline_mode=pl.Buffered(1))


def _nbytes(shape, dtype):
    n = 1
    for s in shape:
        n *= s
    return n * jnp.dtype(dtype).itemsize


def _layer_norm(y, g, b):
    mu = jnp.mean(y, axis=-1, keepdims=True)
    yc = y - mu
    var = jnp.mean(yc * yc, axis=-1, keepdims=True)
    return yc * lax.rsqrt(var + LN_EPS) * g + b


def _sigmoid(z):
    return 1.0 / (1.0 + jnp.exp(-z))


def _softplus(z):
    return jnp.maximum(z, 0.0) + jnp.log(1.0 + jnp.exp(-jnp.abs(z)))


def _split_hi_lo(x):
    hi = x.astype(BF16)
    lo = (x - hi.astype(F32)).astype(BF16)
    return hi, lo


_NT = (((1,), (1,)), ((), ()))
_TN = (((0,), (0,)), ((), ()))


def _gconv_kernel(kind, step_mode, tm, fdim, tiles_per_seq, *refs):
    cw = MXU_COLS
    if step_mode:
        x_ref, w1_ref, wc_ref, w2_ref, g_ref, b_ref, p0_ref, p1_ref, y_ref, st_ref = refs
    else:
        x_ref, w1_ref, wc_ref, w2_ref, g_ref, b_ref, y_ref, st_ref, carry_ref, ubuf_ref = refs

        @pl.when(pl.program_id(0) % tiles_per_seq == 0)
        def _():
            carry_ref[...] = jnp.zeros_like(carry_ref)

    x = x_ref[...]
    xb = x.astype(BF16)
    acc = jnp.zeros((tm, D_MODEL), F32)
    for c in range(fdim // cw):
        lo = c * cw

        def proj(k, lo=lo):
            return jnp.dot(xb, w1_ref[:, k * fdim + lo:k * fdim + lo + cw], preferred_element_type=F32)

        if kind == "ffn":
            u = proj(0)
            other = proj(1)
        else:
            other = proj(0)
            u = proj(1) * proj(2)
        w0 = wc_ref[0:1, lo:lo + cw]
        w1 = wc_ref[1:2, lo:lo + cw]
        w2 = wc_ref[2:3, lo:lo + cw]
        if step_mode:
            z = w0 * p0_ref[:, lo:lo + cw] + w1 * p1_ref[:, lo:lo + cw] + w2 * u
            st_ref[:, lo:lo + cw] = u
        else:
            ubuf_ref[0:8, :] = carry_ref[:, lo:lo + cw]
            ubuf_ref[8:tm + 8, :] = u
            z = w0 * ubuf_ref[6:tm + 6, :] + w1 * ubuf_ref[7:tm + 7, :] + w2 * u
            tail = u[tm - 8:tm, :]
            carry_ref[:, lo:lo + cw] = tail
            st_ref[0, :, lo:lo + cw] = tail
        if kind == "ffn":
            act = z * _sigmoid(z) * other
        else:
            act = other * z
        acc = acc + jnp.dot(act.astype(BF16), w2_ref[lo:lo + cw, :], preferred_element_type=F32)
    y_ref[...] = _layer_norm(DN_ALPHA * x + acc, g_ref[...], b_ref[...])


def _gconv(kind, x, w1, wc, w2, g, b, *, seq_len=None, prev=None, tm=512):
    m = x.shape[0]
    fdim = wc.shape[1]
    nw = w1.shape[1]
    step_mode = prev is not None
    g2, b2 = g.reshape(1, D_MODEL), b.reshape(1, D_MODEL)
    common = [_resident((D_MODEL, nw)), _resident((3, fdim)), _resident((fdim, D_MODEL)),
              _resident((1, D_MODEL)), _resident((1, D_MODEL))]
    wbytes = _nbytes((D_MODEL, nw), BF16) + _nbytes((fdim, D_MODEL), BF16)
    if step_mode:
        tm = m
        kern = functools.partial(_gconv_kernel, kind, True, tm, fdim, 1)
        return pl.pallas_call(
            kern, grid=(1,),
            in_specs=[pl.BlockSpec((tm, D_MODEL), lambda i: (0, 0))] + common
            + [pl.BlockSpec((tm, fdim), lambda i: (0, 0))] * 2,
            out_specs=[pl.BlockSpec((tm, D_MODEL), lambda i: (0, 0)), pl.BlockSpec((tm, fdim), lambda i: (0, 0))],
            out_shape=[jax.ShapeDtypeStruct((m, D_MODEL), F32), jax.ShapeDtypeStruct((m, fdim), F32)],
            compiler_params=_params(1, wbytes + 8 * _nbytes((tm, fdim), F32)),
            name=f"gconv_{kind}_step",
        )(x, w1, wc, w2, g2, b2, prev[0], prev[1])
    tiles_per_seq = seq_len // tm
    n_seq = m // seq_len
    kern = functools.partial(_gconv_kernel, kind, False, tm, fdim, tiles_per_seq)
    return pl.pallas_call(
        kern, grid=(m // tm,),
        in_specs=[pl.BlockSpec((tm, D_MODEL), lambda i: (i, 0))] + common,
        out_specs=[pl.BlockSpec((tm, D_MODEL), lambda i: (i, 0)),
                   pl.BlockSpec((1, 8, fdim), lambda i: (i // tiles_per_seq, 0, 0))],
        out_shape=[jax.ShapeDtypeStruct((m, D_MODEL), F32), jax.ShapeDtypeStruct((n_seq, 8, fdim), F32)],
        scratch_shapes=[pltpu.VMEM((8, fdim), F32), pltpu.VMEM((tm + 8, MXU_COLS), F32)],
        compiler_params=_params(1, wbytes + 6 * _nbytes((tm, D_MODEL), F32) + 12 * _nbytes((tm, MXU_COLS), F32)),
        name=f"gconv_{kind}_seq",
    )(x, w1, wc, w2, g2, b2)


def _proj_kernel(n_cols, outs, parts, x_ref, w_ref, *out_refs):
    cw = MXU_COLS
    if parts == 1:
        xb = x_ref[...].astype(BF16)
    else:
        xb = jnp.concatenate([x_ref[:, j * D_MODEL:(j + 1) * D_MODEL].astype(BF16) for j in range(parts)], axis=0)
    for c in range(n_cols // cw):
        lo = c * cw
        val = None
        for (start, width, scale), o_ref in zip(outs, out_refs):
            if start <= lo < start + width:
                if val is None:
                    val = jnp.dot(xb, w_ref[:, lo:lo + cw], preferred_element_type=F32)
                v = val if scale == 1.0 else val * scale
                o_ref[:, lo - start:lo - start + cw] = v.astype(o_ref.dtype)


def _proj(x, w, outs, *, tm=512, name="proj", row_blocks=None, x_map=None, parts=1):
    if x_map is None:
        tm = min(tm, x.shape[0])
        row_blocks, x_map = x.shape[0] // tm, (lambda i: (i, 0))
    m = row_blocks * tm
    n = w.shape[1]
    spec = tuple((s, wd, sc) for s, wd, sc, _ in outs)
    kern = functools.partial(_proj_kernel, n, spec, parts)
    obytes = sum(_nbytes((tm, wd), dt) for _, wd, _, dt in outs)
    return pl.pallas_call(
        kern, grid=(row_blocks,),
        in_specs=[pl.BlockSpec((tm // parts, parts * D_MODEL), x_map), _resident((D_MODEL, n))],
        out_specs=[pl.BlockSpec((tm, wd), lambda i: (i, 0)) for _, wd, _, _ in outs],
        out_shape=[jax.ShapeDtypeStruct((m, wd), dt) for _, wd, _, dt in outs],
        compiler_params=_params(1, _nbytes((D_MODEL, n), BF16) + 2 * obytes + 4 * _nbytes((tm, D_MODEL), F32)),
        name=name,
    )(x, w)


def _outln_kernel(a_ref, w_ref, x_ref, g_ref, b_ref, y_ref):
    mix = jnp.dot(a_ref[...].astype(BF16), w_ref[...], preferred_element_type=F32)
    y_ref[...] = _layer_norm(DN_ALPHA * x_ref[...] + mix, g_ref[...], b_ref[...])


def _out_ln(a, w, x, g, b, *, tm=512):
    m, kin = a.shape
    tm = min(tm, m)
    return pl.pallas_call(
        _outln_kernel, grid=(m // tm,),
        in_specs=[pl.BlockSpec((tm, kin), lambda i: (i, 0)), _resident((kin, D_MODEL)),
                  pl.BlockSpec((tm, D_MODEL), lambda i: (i, 0)), _resident((1, D_MODEL)), _resident((1, D_MODEL))],
        out_specs=pl.BlockSpec((tm, D_MODEL), lambda i: (i, 0)),
        out_shape=jax.ShapeDtypeStruct((m, D_MODEL), F32),
        compiler_params=_params(1, _nbytes((kin, D_MODEL), BF16) + 8 * _nbytes((tm, D_MODEL), F32)),
        name="out_ln",
    )(a, w, x, g.reshape(1, D_MODEL), b.reshape(1, D_MODEL))


def _sb_attn_kernel(tq, tk, nkb, bias_ref, q_ref, k_ref, v_ref, o_ref, k2_ref, v2_ref, carry_ref, acc_ref):
    hp = pl.program_id(1)
    i = pl.program_id(2)
    hd = SB_HEAD_DIM

    @pl.when(i == 0)
    def _():
        lane = lax.broadcasted_iota(jnp.int32, (tk, 2 * hd), 1)

        def build(j, c):
            r0 = pl.multiple_of(j * tk, tk)
            for src, dst in ((k_ref, k2_ref), (v_ref, v2_ref)):
                blk = src[pl.ds(r0, tk), :]
                zero = jnp.zeros_like(blk)
                dst[j, 0:tk, :] = jnp.where(lane < hd, blk, zero)
                dst[j, tk:2 * tk, :] = jnp.where(lane >= hd, blk, zero)
            return c

        lax.fori_loop(0, nkb, build, 0)

    lane = lax.broadcasted_iota(jnp.int32, (tq, 2 * tk), 1)
    row = lax.broadcasted_iota(jnp.int32, (tq, 2 * tk), 0)
    bias = jnp.where(lane < tk, bias_ref[2 * hp], bias_ref[2 * hp + 1])
    causal = (lane & (tk - 1)) < row
    ri = lax.broadcasted_iota(jnp.int32, (2 * tk, 2 * tk), 0)
    ci = lax.broadcasted_iota(jnp.int32, (2 * tk, 2 * tk), 1)
    same_head = (ri // tk) == (ci // tk)
    suffix_ones = jnp.where(same_head, jnp.where(ri >= ci, 1.0, 0.0), 0.0).astype(BF16)
    q = q_ref[...]
    carry_ref[...] = jnp.zeros_like(carry_ref)
    acc_ref[...] = jnp.zeros_like(acc_ref)

    def step(j, masked):
        z = lax.dot_general(q, k2_ref[j], _NT, preferred_element_type=F32) + bias
        sp = _softplus(z)
        if masked:
            sp = jnp.where(causal, sp, 0.0)
        hi, lo = _split_hi_lo(sp)
        c_loc = (jnp.dot(hi, suffix_ones, preferred_element_type=F32)
                 + jnp.dot(lo, suffix_ones, preferred_element_type=F32))
        w = jnp.exp(z - c_loc - carry_ref[...])
        if masked:
            w = jnp.where(causal, w, 0.0)
        acc_ref[...] += jnp.dot(w.astype(BF16), v2_ref[j], preferred_element_type=F32)
        tot = jnp.concatenate([jnp.broadcast_to(c_loc[:, 0:1], (tq, tk)),
                               jnp.broadcast_to(c_loc[:, tk:tk + 1], (tq, tk))], axis=1)
        carry_ref[...] += tot

    step(i, True)

    def body(s, c):
        step(i - 1 - s, False)
        return c

    lax.fori_loop(0, i, body, 0)
    o_ref[...] = acc_ref[...].astype(o_ref.dtype)


def _sb_attention_prompt(qb, kb, vb, bias, n_seq, seq_len):
    tq = tk = 128
    nq = seq_len // tq
    kern = functools.partial(_sb_attn_kernel, tq, tk, seq_len // tk)
    scratch = 2 * _nbytes((seq_len * 2, 128), BF16) + _nbytes((tq, 2 * tk), F32) + _nbytes((tq, 128), F32)
    return pl.pallas_call(
        kern, grid=(n_seq, SB_HEADS // 2, nq),
        in_specs=[pl.BlockSpec(memory_space=pltpu.SMEM),
                  pl.BlockSpec((tq, 128), lambda b, h, i: (b * nq + i, h)),
                  pl.BlockSpec((seq_len, 128), lambda b, h, i: (b, h)),
                  pl.BlockSpec((seq_len, 128), lambda b, h, i: (b, h))],
        out_specs=pl.BlockSpec((tq, 128), lambda b, h, i: (b * nq + i, h)),
        out_shape=jax.ShapeDtypeStruct(qb.shape, BF16),
        scratch_shapes=[pltpu.VMEM((seq_len // tk, 2 * tk, 128), BF16), pltpu.VMEM((seq_len // tk, 2 * tk, 128), BF16),
                        pltpu.VMEM((tq, 2 * tk), F32), pltpu.VMEM((tq, 128), F32)],
        compiler_params=_params(3, scratch + 4 * _nbytes((seq_len, 128), BF16) + (4 << 20)),
        name="sb_attn_prompt",
    )(bias, qb, kb, vb)


def _sb_sample_kernel(n_pages, pt_ref, q_ref, bias_ref, k_ref, v_ref, o_ref, qrows_ref, acc_ref, carry_ref):
    p = pl.program_id(1)
    nh = SB_HEADS
    width = SB_HEADS * SB_HEAD_DIM
    head_of_lane = lax.broadcasted_iota(jnp.int32, (nh, width), 1) // SB_HEAD_DIM
    head_of_row = lax.broadcasted_iota(jnp.int32, (nh, width), 0)
    own = head_of_lane == head_of_row

    @pl.when(p == 0)
    def _():
        qrow = q_ref[0] * (SB_HEAD_DIM ** -0.5)
        qrows_ref[...] = jnp.where(own, jnp.broadcast_to(qrow, (nh, width)), 0.0).astype(BF16)
        acc_ref[...] = jnp.zeros_like(acc_ref)
        carry_ref[...] = jnp.zeros_like(carry_ref)

    kb = k_ref[0].astype(BF16)
    vb = v_ref[0].astype(BF16)
    z = lax.dot_general(qrows_ref[...], kb, _NT, preferred_element_type=F32) + bias_ref[...]
    sp = _softplus(z)
    ri = lax.broadcasted_iota(jnp.int32, (2 * PAGE_SIZE, PAGE_SIZE), 0) & (PAGE_SIZE - 1)
    ci = lax.broadcasted_iota(jnp.int32, (2 * PAGE_SIZE, PAGE_SIZE), 1)
    suffix_ones = jnp.where(ri >= ci, 1.0, 0.0).astype(BF16)
    hi, lo = _split_hi_lo(sp)
    c_loc = jnp.dot(jnp.concatenate([hi, lo], axis=1), suffix_ones, preferred_element_type=F32)
    w = jnp.exp(z - c_loc - carry_ref[...])
    acc_ref[...] += jnp.dot(w.astype(BF16), vb, preferred_element_type=F32)
    carry_ref[...] += jnp.broadcast_to(c_loc[:, 0:1], (nh, PAGE_SIZE))

    @pl.when(p == n_pages - 1)
    def _():
        o_ref[0] = jnp.sum(jnp.where(own, acc_ref[...], 0.0), axis=0, keepdims=True)


def _sb_attention_sample(q, cache_k, cache_v, page_table, bias):
    bsz, n_pages = page_table.shape
    width = SB_HEADS * SB_HEAD_DIM
    n_pool = cache_k.shape[0]
    ck = cache_k.reshape(n_pool, PAGE_SIZE, width)
    cv = cache_v.reshape(n_pool, PAGE_SIZE, width)
    page_spec = pl.BlockSpec((1, PAGE_SIZE, width), lambda b, p, pt: (pt[b, n_pages - 1 - p], 0, 0))
    grid_spec = pltpu.PrefetchScalarGridSpec(
        num_scalar_prefetch=1, grid=(bsz, n_pages),
        in_specs=[pl.BlockSpec((1, 1, width), lambda b, p, pt: (b, 0, 0)),
                  pl.BlockSpec((SB_HEADS, 1), lambda b, p, pt: (0, 0)),
                  page_spec, page_spec],
        out_specs=pl.BlockSpec((1, 1, width), lambda b, p, pt: (b, 0, 0)),
        scratch_shapes=[pltpu.VMEM((SB_HEADS, width), BF16), pltpu.VMEM((SB_HEADS, width), F32),
                        pltpu.VMEM((SB_HEADS, PAGE_SIZE), F32)])
    out = pl.pallas_call(
        functools.partial(_sb_sample_kernel, n_pages), grid_spec=grid_spec,
        out_shape=jax.ShapeDtypeStruct((bsz, 1, width), F32),
        compiler_params=_params(2, 6 * _nbytes((PAGE_SIZE, width), F32)),
        name="sb_attn_sample",
    )(page_table, q.reshape(bsz, 1, width), bias.reshape(SB_HEADS, 1), ck, cv)
    return out.reshape(bsz, width)


def _gla_in_kernel(x_ref, w_ref, wg_ref, bg_ref, q_ref, k_ref, v_ref, r_ref, la_ref):
    cw = MXU_COLS
    xb = x_ref[...].astype(BF16)

    def cols(lo, n):
        return jnp.dot(xb, w_ref[:, lo:lo + n], preferred_element_type=F32)

    for c in range(GLA_QK // cw):
        q_ref[:, c * cw:(c + 1) * cw] = cols(c * cw, cw) * (GLA_DK ** -0.5)
        k_ref[:, c * cw:(c + 1) * cw] = cols(GLA_QK + c * cw, cw)
    for c in range(GLA_V // cw):
        v_ref[:, c * cw:(c + 1) * cw] = cols(2 * GLA_QK + c * cw, cw)
        r_ref[:, c * cw:(c + 1) * cw] = cols(2 * GLA_QK + GLA_V + c * cw, cw)
    g_low = cols(2 * GLA_QK + 2 * GLA_V, 128).astype(BF16)
    u = jnp.dot(g_low, wg_ref[...], preferred_element_type=F32) + bg_ref[...]
    la_ref[...] = -_softplus(-u) * (1.0 / GLA_TAU)


def _gla_in(x, w_pad, wg_pad, bg, *, tm=512):
    m = x.shape[0]
    tm = min(tm, m)
    n = w_pad.shape[1]
    widths = (GLA_QK, GLA_QK, GLA_V, GLA_V, GLA_QK)
    return pl.pallas_call(
        _gla_in_kernel, grid=(m // tm,),
        in_specs=[pl.BlockSpec((tm, D_MODEL), lambda i: (i, 0)), _resident((D_MODEL, n)),
                  _resident((128, GLA_QK)), _resident((1, GLA_QK))],
        out_specs=[pl.BlockSpec((tm, wd), lambda i: (i, 0)) for wd in widths],
        out_shape=[jax.ShapeDtypeStruct((m, wd), F32) for wd in widths],
        compiler_params=_params(1, _nbytes((D_MODEL, n), BF16) + 12 * _nbytes((tm, D_MODEL), F32)),
        name="gla_in",
    )(x, w_pad, wg_pad, bg.reshape(1, GLA_QK))


def _gla_kernel(t_rows, n_t, q_ref, k_ref, la_ref, v_ref, r_ref, gn_ref, o_ref, s_ref, st_ref):
    t = pl.program_id(2)
    ch = GLA_CHUNK

    @pl.when(t == 0)
    def _():
        st_ref[...] = jnp.zeros_like(st_ref)

    ri = lax.broadcasted_iota(jnp.int32, (ch, ch), 0)
    ci = lax.broadcasted_iota(jnp.int32, (ch, ch), 1)
    tri = ci <= ri
    prefix_ones = jnp.where(tri, 1.0, 0.0).astype(BF16)
    prefix_ones2 = jnp.concatenate([prefix_ones, prefix_ones], axis=1)
    gn = gn_ref[...]
    for c in range(t_rows // ch):
        sl = slice(c * ch, (c + 1) * ch)
        hi, lo = _split_hi_lo(la_ref[sl, :])
        b = jnp.dot(prefix_ones2, jnp.concatenate([hi, lo], axis=0), preferred_element_type=F32)
        mid = b[ch // 2 - 1:ch // 2, :]
        last = b[ch - 1:ch, :]
        qs = q_ref[sl, :] * jnp.exp(b - mid)
        ks = k_ref[sl, :] * jnp.exp(mid - b)
        scores = lax.dot_general(qs.astype(BF16), ks.astype(BF16), _NT, preferred_element_type=F32)
        scores = jnp.where(tri, scores, 0.0)
        vb = v_ref[sl, :].astype(BF16)
        st = st_ref[...]
        qe = (qs * jnp.exp(mid)).astype(BF16)
        o = (jnp.dot(scores.astype(BF16), vb, preferred_element_type=F32)
             + lax.dot_general(qe, st.astype(BF16), _NT, preferred_element_type=F32))
        kd = (ks * jnp.exp(last - mid)).astype(BF16)
        st_ref[...] = jnp.exp(last) * st + lax.dot_general(vb, kd, _TN, preferred_element_type=F32)
        o = o * lax.rsqrt(jnp.mean(o * o, axis=-1, keepdims=True) + LN_EPS) * gn
        r = r_ref[sl, :]
        o_ref[sl, :] = (o * (r * _sigmoid(r))).astype(o_ref.dtype)

    @pl.when(t == n_t - 1)
    def _():
        s_ref[0, 0] = st_ref[...].T


def _gla_prompt(q, k, la, v, r, g_norm, n_seq, seq_len, *, t_rows=512):
    n_t = seq_len // t_rows
    kern = functools.partial(_gla_kernel, t_rows, n_t)
    qk_spec = pl.BlockSpec((t_rows, GLA_DK), lambda b, h, t: (b * n_t + t, h))
    v_spec = pl.BlockSpec((t_rows, GLA_DV), lambda b, h, t: (b * n_t + t, h))
    return pl.pallas_call(
        kern, grid=(n_seq, GLA_HEADS, n_t),
        in_specs=[qk_spec, qk_spec, qk_spec, v_spec, v_spec, pl.BlockSpec((1, GLA_DV), lambda b, h, t: (0, 0))],
        out_specs=[v_spec, pl.BlockSpec((1, 1, GLA_DK, GLA_DV), lambda b, h, t: (b, h, 0, 0))],
        out_shape=[jax.ShapeDtypeStruct((n_seq * seq_len, GLA_V), BF16),
                   jax.ShapeDtypeStruct((n_seq, GLA_HEADS, GLA_DK, GLA_DV), F32)],
        scratch_shapes=[pltpu.VMEM((GLA_DV, GLA_DK), F32)],
        compiler_params=_params(3, 16 * _nbytes((t_rows, GLA_DV), F32)),
        name="gla_prompt",
    )(q, k, la, v, r, g_norm.reshape(1, GLA_DV))


def _gla_sample_kernel(q_ref, k_ref, la_ref, v_ref, r_ref, gn_ref, s0_ref, o_ref, s_ref):
    gn = gn_ref[...]
    for h in range(GLA_HEADS):
        a = jnp.exp(la_ref[0, h])
        vrow = v_ref[0, :, h * GLA_DV:(h + 1) * GLA_DV]
        s_new = a * s0_ref[0, h] + k_ref[0, h] * vrow
        s_ref[0, h] = s_new
        o = jnp.sum(q_ref[0, h] * s_new, axis=0, keepdims=True)
        o = o * lax.rsqrt(jnp.mean(o * o, axis=-1, keepdims=True) + LN_EPS) * gn
        r = r_ref[0, :, h * GLA_DV:(h + 1) * GLA_DV]
        o_ref[0, :, h * GLA_DV:(h + 1) * GLA_DV] = o * (r * _sigmoid(r))


def _gla_sample(q, k, la, v, r, g_norm, s0):
    bsz = q.shape[0]
    col = lambda t: t.reshape(bsz, GLA_HEADS, GLA_DK, 1)
    col_spec = pl.BlockSpec((1, GLA_HEADS, GLA_DK, 1), lambda b: (b, 0, 0, 0))
    row_spec = pl.BlockSpec((1, 1, GLA_V), lambda b: (b, 0, 0))
    st_spec = pl.BlockSpec((1, GLA_HEADS, GLA_DK, GLA_DV), lambda b: (b, 0, 0, 0))
    o, s = pl.pallas_call(
        _gla_sample_kernel, grid=(bsz,),
        in_specs=[col_spec, col_spec, col_spec, row_spec, row_spec,
                  pl.BlockSpec((1, GLA_DV), lambda b: (0, 0)), st_spec],
        out_specs=[row_spec, st_spec],
        out_shape=[jax.ShapeDtypeStruct((bsz, 1, GLA_V), F32), jax.ShapeDtypeStruct(s0.shape, F32)],
        compiler_params=_params(1, 8 * _nbytes((GLA_HEADS, GLA_DK, GLA_DV), F32)),
        name="gla_sample",
    )(col(q), col(k), col(la), v.reshape(bsz, 1, GLA_V), r.reshape(bsz, 1, GLA_V), g_norm.reshape(1, GLA_DV), s0)
    return o.reshape(bsz, GLA_V), s


def _dsw_attn_kernel(d, bpc, slope_ref, q_ref, kp_ref, kc_ref, vp_ref, vc_ref, o_ref, lse_ref):
    t = pl.program_id(1)
    sb = pl.program_id(2)
    qb = DSW_QB
    hd = DSW_HEAD_DIM
    has_prev = jnp.minimum(t + (sb & (bpc - 1)), 1)
    row = lax.broadcasted_iota(jnp.int32, (qb, 2 * qb), 0)
    col = lax.broadcasted_iota(jnp.int32, (qb, 2 * qb), 1)
    steps_back = row + qb - col
    first_col = qb - has_prev * qb
    valid = (steps_back >= 0) & (steps_back <= qb) & (col >= first_col)
    dist = steps_back.astype(F32) * float(d)
    q = q_ref[...]
    kk = jnp.concatenate([kp_ref[...], kc_ref[...]], axis=0)
    vv = jnp.concatenate([vp_ref[...], vc_ref[...]], axis=0)
    for h in range(DSW_HEADS):
        hs = slice(h * hd, (h + 1) * hd)
        s = lax.dot_general(q[:, hs], kk[:, hs], _NT, preferred_element_type=F32) - slope_ref[h] * dist
        s = jnp.where(valid, s, -jnp.inf)
        m = jnp.max(s, axis=-1, keepdims=True)
        p = jnp.exp(s - m)
        l = jnp.sum(p, axis=-1, keepdims=True)
        o = jnp.dot(p.astype(BF16), vv[:, hs], preferred_element_type=F32) / l
        o_ref[:, hs] = o
        lse_ref[:, hs] = jnp.broadcast_to(m + jnp.log(l), (qb, hd))


def _dsw_attention_prompt(d, qc, kc, vc, slopes, n_seq, seq_len):
    qb = DSW_QB
    n_tiles = seq_len // DSW_TILE
    nsb = DSW_TILE // qb
    bpc = nsb // d

    def cur(b, t, sb):
        return ((b * n_tiles + t) * nsb + sb, 0)

    def prev(b, t, sb):
        first = (sb & (bpc - 1)) == 0
        back = jnp.where(first, jnp.where(t > 0, nsb - bpc + 1, 0), 1)
        return ((b * n_tiles + t) * nsb + sb - back, 0)

    def nat(b, t, sb):
        return ((b * n_tiles + t) * bpc + (sb & (bpc - 1)), sb // bpc)

    blk = lambda im: pl.BlockSpec((qb, DSW_INNER), im)
    m = n_seq * seq_len
    o, lse = pl.pallas_call(
        functools.partial(_dsw_attn_kernel, d, bpc), grid=(n_seq, n_tiles, nsb),
        in_specs=[pl.BlockSpec(memory_space=pltpu.SMEM), blk(cur), blk(prev), blk(cur), blk(prev), blk(cur)],
        out_specs=[blk(nat), blk(nat)],
        out_shape=[jax.ShapeDtypeStruct((m // d, d * DSW_INNER), F32)] * 2,
        compiler_params=_params(3, 32 * _nbytes((qb, DSW_INNER), F32)),
        name=f"dsw_attn_prompt_d{d}",
    )(slopes, qc, kc, kc, vc, vc)
    return o.reshape(m, DSW_INNER), lse.reshape(m, DSW_INNER)


def _dsw_outln_kernel(o0_ref, o1_ref, o2_ref, l0_ref, l1_ref, l2_ref, w_ref, x_ref, g_ref, b_ref, y_ref):
    l0, l1, l2 = l0_ref[...], l1_ref[...], l2_ref[...]
    m = jnp.maximum(jnp.maximum(l0, l1), l2)
    e0, e1, e2 = jnp.exp(l0 - m), jnp.exp(l1 - m), jnp.exp(l2 - m)
    o = (o0_ref[...] * e0 + o1_ref[...] * e1 + o2_ref[...] * e2) / (e0 + e1 + e2)
    mix = jnp.dot(o.astype(BF16), w_ref[...], preferred_element_type=F32)
    y_ref[...] = _layer_norm(DN_ALPHA * x_ref[...] + mix, g_ref[...], b_ref[...])


def _dsw_out_ln(os_, lses, w, x, g, b, *, tm=512):
    m = x.shape[0]
    part = pl.BlockSpec((tm, DSW_INNER), lambda i: (i, 0))
    row = pl.BlockSpec((tm, D_MODEL), lambda i: (i, 0))
    return pl.pallas_call(
        _dsw_outln_kernel, grid=(m // tm,),
        in_specs=[part] * 6 + [_resident((DSW_INNER, D_MODEL)), row, _resident((1, D_MODEL)), _resident((1, D_MODEL))],
        out_specs=row,
        out_shape=jax.ShapeDtypeStruct((m, D_MODEL), F32),
        compiler_params=_params(1, 24 * _nbytes((tm, DSW_INNER), F32) + 8 * _nbytes((tm, D_MODEL), F32)),
        name="dsw_out_ln",
    )(*os_, *lses, w, x, g.reshape(1, D_MODEL), b.reshape(1, D_MODEL))


def _dsw_sample_kernel(slope_ref, qkv_ref, r0_ref, r1_ref, r2_ref, o_ref):
    nh = DSW_HEADS
    hd = DSW_HEAD_DIM
    inner = DSW_INNER
    head_of_lane = lax.broadcasted_iota(jnp.int32, (nh, inner), 1) // hd
    head_of_row = lax.broadcasted_iota(jnp.int32, (nh, inner), 0)
    own = head_of_lane == head_of_row
    key_idx = lax.broadcasted_iota(jnp.int32, (nh, DSW_QB), 1)
    parts = []
    for g, ((_, d), rows_ref) in enumerate(zip(DSW_GROUPS, (r0_ref, r1_ref, r2_ref))):
        base = g * 3 * inner
        q = qkv_ref[0, :, base:base + inner] * (hd ** -0.5)
        k_new = qkv_ref[0, :, base + inner:base + 2 * inner]
        v_new = qkv_ref[0, :, base + 2 * inner:base + 3 * inner]
        qrows = jnp.where(own, jnp.broadcast_to(q, (nh, inner)), 0.0)
        kb = rows_ref[0, :, 0:inner].astype(BF16)
        vb = rows_ref[0, :, inner:2 * inner].astype(BF16)
        slope = jnp.concatenate([jnp.full((1, 1), slope_ref[g, h], F32) for h in range(nh)], axis=0)
        dist = ((DSW_QB - key_idx) * d).astype(F32)
        s = lax.dot_general(qrows.astype(BF16), kb, _NT, preferred_element_type=F32) - slope * dist
        s_new = jnp.sum(qrows * k_new, axis=-1, keepdims=True)
        m = jnp.maximum(jnp.max(s, axis=-1, keepdims=True), s_new)
        p = jnp.exp(s - m)
        p_new = jnp.exp(s_new - m)
        l = jnp.sum(p, axis=-1, keepdims=True) + p_new
        num = jnp.dot(p.astype(BF16), vb, preferred_element_type=F32) + p_new * v_new
        parts.append((m, l, num))
    m_all = jnp.maximum(jnp.maximum(parts[0][0], parts[1][0]), parts[2][0])
    den = sum(l * jnp.exp(m - m_all) for m, l, _ in parts)
    num = sum(n * jnp.exp(m - m_all) for m, _, n in parts)
    o_ref[0] = jnp.sum(jnp.where(own, num / den, 0.0), axis=0, keepdims=True)


def _dsw_attention_sample(qkv, bufs, slopes):
    bsz = qkv.shape[0]
    n = qkv.shape[1]
    row_specs, rows = [], []
    for (w, d), buf in zip(DSW_GROUPS, bufs):
        rows.append(buf.reshape(bsz, w // d, d * 2 * DSW_INNER))
        row_specs.append(pl.BlockSpec((1, w // d, 2 * DSW_INNER), lambda b: (b, 0, 0)))
    out = pl.pallas_call(
        _dsw_sample_kernel, grid=(bsz,),
        in_specs=[pl.BlockSpec(memory_space=pltpu.SMEM), pl.BlockSpec((1, 1, n), lambda b: (b, 0, 0))] + row_specs,
        out_specs=pl.BlockSpec((1, 1, DSW_INNER), lambda b: (b, 0, 0)),
        out_shape=jax.ShapeDtypeStruct((bsz, 1, DSW_INNER), F32),
        compiler_params=_params(1, 16 * _nbytes((DSW_QB, 2 * DSW_INNER), F32)),
        name="dsw_attn_sample",
    )(slopes, qkv.reshape(bsz, 1, n), *rows)
    return out.reshape(bsz, DSW_INNER)


def _shift_kernel(t_rows, n_t, cur_ref, nxt_ref, new_ref, o_ref, stage_ref):
    last = pl.program_id(1) == n_t - 1
    stage_ref[0:t_rows, :] = cur_ref[0]
    stage_ref[t_rows:t_rows + 8, :] = jnp.where(last, jnp.broadcast_to(new_ref[0], (8, new_ref.shape[2])), nxt_ref[0])
    o_ref[0] = stage_ref[1:t_rows + 1, :]


def _shift_append(buf, new_rows):
    bsz, w, c = buf.shape
    t_rows = min(w, 512)
    n_t = w // t_rows
    per = t_rows // 8
    return pl.pallas_call(
        functools.partial(_shift_kernel, t_rows, n_t), grid=(bsz, n_t),
        in_specs=[pl.BlockSpec((1, t_rows, c), lambda b, t: (b, t, 0)),
                  pl.BlockSpec((1, 8, c), lambda b, t: (b, jnp.minimum((t + 1) * per, w // 8 - 1), 0)),
                  pl.BlockSpec((1, 1, c), lambda b, t: (b, 0, 0))],
        out_specs=pl.BlockSpec((1, t_rows, c), lambda b, t: (b, t, 0)),
        out_shape=jax.ShapeDtypeStruct(buf.shape, buf.dtype),
        scratch_shapes=[pltpu.VMEM((t_rows + 8, c), buf.dtype)],
        compiler_params=_params(2, 6 * _nbytes((t_rows, c), buf.dtype)),
        name="shift_append",
    )(buf, buf, new_rows.reshape(bsz, 1, c))


def kernel(x_prompt, x_sample, cache_sc_conv, cache_sb_k, cache_sb_v, state_gla, cache_dsw_kv0, cache_dsw_kv1,
           cache_dsw_kv2, state_ffn_conv, page_table, ln_g, ln_b, w_sc_in, w_sc_conv, w_sc_out, w_sb_qkv, w_sb_out,
           b_sb, w_gla_in, w_gla_gate_up, b_gla_gate, g_gla_norm, w_gla_out, w_dsw_qkv, w_dsw_out, w_ffn_up,
           w_ffn_conv, w_ffn_down):
    n_seq, seq_len, _ = x_prompt.shape
    bsz = x_sample.shape[0]
    xp = x_prompt.reshape(n_seq * seq_len, D_MODEL)
    xs = x_sample.reshape(bsz, D_MODEL)
    bf = lambda w: w.astype(BF16)
    ffn_p, ffn_s = [], []

    def ffn(layer, xp, xs):
        w_up, w_down = bf(w_ffn_up[layer]), bf(w_ffn_down[layer])
        g, b = ln_g[layer, 1], ln_b[layer, 1]
        xp, st_p = _gconv("ffn", xp, w_up, w_ffn_conv[layer], w_down, g, b, seq_len=seq_len)
        prev = state_ffn_conv[layer]
        xs, u_s = _gconv("ffn", xs, w_up, w_ffn_conv[layer], w_down, g, b, prev=(prev[:, 0], prev[:, 1]))
        ffn_p.append(st_p[:, 6:8])
        ffn_s.append(jnp.stack([prev[:, 1], u_s], axis=1))
        return xp, xs

    w_in, w_out = bf(w_sc_in), bf(w_sc_out)
    xp, sc_st = _gconv("sc", xp, w_in, w_sc_conv, w_out, ln_g[0, 0], ln_b[0, 0], seq_len=seq_len)
    xs, sc_u = _gconv("sc", xs, w_in, w_sc_conv, w_out, ln_g[0, 0], ln_b[0, 0],
                      prev=(cache_sc_conv[:, 0], cache_sc_conv[:, 1]))
    sc_p = sc_st[:, 6:8]
    sc_s = jnp.stack([cache_sc_conv[:, 1], sc_u], axis=1)
    xp, xs = ffn(0, xp, xs)

    w_qkv, w_out = bf(w_sb_qkv), bf(w_sb_out)
    width = SB_HEADS * SB_HEAD_DIM
    scale = SB_HEAD_DIM ** -0.5
    qb, kf, vf, kb, vb = _proj(xp, w_qkv, [(0, width, scale, BF16), (width, width, 1.0, F32),
                                           (2 * width, width, 1.0, F32), (width, width, 1.0, BF16),
                                           (2 * width, width, 1.0, BF16)], name="sb_qkv_prompt")
    o = _sb_attention_prompt(qb, kb, vb, b_sb, n_seq, seq_len)
    sb_k_p = kf.reshape(n_seq, seq_len, SB_HEADS, SB_HEAD_DIM)
    sb_v_p = vf.reshape(n_seq, seq_len, SB_HEADS, SB_HEAD_DIM)
    xp = _out_ln(o, w_out, xp, ln_g[1, 0], ln_b[1, 0])
    qs, ks, vs = _proj(xs, w_qkv, [(0, width, 1.0, F32), (width, width, 1.0, F32), (2 * width, width, 1.0, F32)],
                       name="sb_qkv_sample")
    o = _sb_attention_sample(qs, cache_sb_k, cache_sb_v, page_table, b_sb)
    sb_k_s = ks.reshape(bsz, 1, SB_HEADS, SB_HEAD_DIM)
    sb_v_s = vs.reshape(bsz, 1, SB_HEADS, SB_HEAD_DIM)
    xs = _out_ln(o, w_out, xs, ln_g[1, 0], ln_b[1, 0])
    xp, xs = ffn(1, xp, xs)

    n_in = w_gla_in.shape[1]
    w_in = bf(jnp.pad(w_gla_in, ((0, 0), (0, 2 * GLA_QK + 2 * GLA_V + 128 - n_in))))
    w_gate = bf(jnp.pad(w_gla_gate_up, ((0, 128 - GLA_RANK), (0, 0))))
    w_out = bf(w_gla_out)
    q, k, v, r, la = _gla_in(xp, w_in, w_gate, b_gla_gate)
    o, gla_p = _gla_prompt(q, k, la, v, r, g_gla_norm, n_seq, seq_len)
    xp = _out_ln(o, w_out, xp, ln_g[2, 0], ln_b[2, 0])
    q, k, v, r, la = _gla_in(xs, w_in, w_gate, b_gla_gate)
    o, gla_s = _gla_sample(q, k, la, v, r, g_gla_norm, state_gla)
    xs = _out_ln(o, w_out, xs, ln_g[2, 0], ln_b[2, 0])
    xp, xs = ffn(2, xp, xs)

    w_qkv, w_out = bf(w_dsw_qkv), bf(w_dsw_out)
    ng = len(DSW_GROUPS)
    n_heads = ng * DSW_HEADS
    slopes = (2.0 ** (-8.0 * jnp.arange(1, n_heads + 1, dtype=F32) / n_heads)).reshape(ng, DSW_HEADS)
    m = n_seq * seq_len
    tm = 512
    os_, lses = [], []
    for g, (_, d) in enumerate(DSW_GROUPS):
        rows_per_class = DSW_TILE // d
        parts = max(1, tm // rows_per_class)
        steps_per_tile = DSW_TILE // tm
        x_map = (lambda i: (i, 0)) if d == 1 else (lambda i: (i // steps_per_tile, i % steps_per_tile))
        base = g * 3 * DSW_INNER
        qc, kc, vc = _proj(xp.reshape(m // d, d * D_MODEL), w_qkv,
                           [(base, DSW_INNER, DSW_HEAD_DIM ** -0.5, BF16), (base + DSW_INNER, DSW_INNER, 1.0, BF16),
                            (base + 2 * DSW_INNER, DSW_INNER, 1.0, BF16)],
                           tm=tm, name=f"dsw_qkv_prompt_d{d}", row_blocks=m // tm, x_map=x_map, parts=parts)
        o, lse = _dsw_attention_prompt(d, qc, kc, vc, slopes[g], n_seq, seq_len)
        os_.append(o)
        lses.append(lse)
    w_max = max(w for w, _ in DSW_GROUPS)
    per_seq, tail = seq_len // tm, w_max // tm
    kv_tail = _proj(xp, w_qkv, [(g * 3 * DSW_INNER + DSW_INNER, 2 * DSW_INNER, 1.0, F32) for g in range(ng)],
                    tm=tm, name="dsw_kv_tail", row_blocks=n_seq * tail,
                    x_map=lambda i: ((i // tail) * per_seq + per_seq - tail + i % tail, 0))
    dsw_p = [kv.reshape(n_seq, w_max, 2, DSW_HEADS, DSW_HEAD_DIM)[:, w_max - w:]
             for kv, (w, _) in zip(kv_tail, DSW_GROUPS)]
    xp = _dsw_out_ln(os_, lses, w_out, xp, ln_g[3, 0], ln_b[3, 0])
    bufs = [cache_dsw_kv0, cache_dsw_kv1, cache_dsw_kv2]
    bufs = [b_.reshape(bsz, b_.shape[1], 2 * DSW_INNER) for b_ in bufs]
    (qkv,) = _proj(xs, w_qkv, [(0, w_qkv.shape[1], 1.0, F32)], name="dsw_qkv_sample")
    o = _dsw_attention_sample(qkv, bufs, slopes)
    xs = _out_ln(o, w_out, xs, ln_g[3, 0], ln_b[3, 0])
    dsw_s = []
    for g, (w, _) in enumerate(DSW_GROUPS):
        new_kv = qkv[:, g * 3 * DSW_INNER + DSW_INNER:(g + 1) * 3 * DSW_INNER]
        dsw_s.append(_shift_append(bufs[g], new_kv).reshape(bsz, w, 2, DSW_HEADS, DSW_HEAD_DIM))
    xp, xs = ffn(3, xp, xs)

    return (xp.reshape(n_seq, seq_len, D_MODEL), xs.reshape(bsz, 1, D_MODEL), sc_p, sc_s, sb_k_p, sb_v_p, sb_k_s,
            sb_v_s, gla_p, gla_s, dsw_p[0], dsw_p[1], dsw_p[2], dsw_s[0], dsw_s[1], dsw_s[2],
            jnp.stack(ffn_p), jnp.stack(ffn_s))
```

```python
import functools

import jax
import jax.numpy as jnp
from jax import lax
from jax.experimental import pallas as pl
from jax.experimental.pallas import tpu as pltpu

F32 = jnp.float32
BF16 = jnp.bfloat16

D_MODEL = 1024
DEPTH = 4
DN_ALPHA = (2.0 * DEPTH) ** 0.25
LN_EPS = 1e-5
PAGE_SIZE = 128

SB_HEADS = 16
SB_HEAD_DIM = 64

GLA_HEADS = 4
GLA_DK = 128
GLA_DV = 256
GLA_RANK = 16
GLA_TAU = 16.0
GLA_QK = GLA_HEADS * GLA_DK
GLA_V = GLA_HEADS * GLA_DV
GLA_CHUNK = 128

DSW_GROUPS = ((128, 1), (512, 4), (2048, 16))
DSW_HEADS = 8
DSW_HEAD_DIM = 64
DSW_INNER = DSW_HEADS * DSW_HEAD_DIM
DSW_TILE = 2048
DSW_QB = 128

D_FF = 2816
MXU_COLS = 256
V7X_VMEM_BYTES = 64 * 1024 * 1024
VMEM_CAP = V7X_VMEM_BYTES - 6 * 1024 * 1024


def _vmem_limit(block_bytes):
    return int(min(VMEM_CAP, block_bytes * 5 // 4 + (8 << 20)))


def _params(n_grid, block_bytes):
    return pltpu.CompilerParams(dimension_semantics=("arbitrary",) * n_grid,
                                vmem_limit_bytes=_vmem_limit(block_bytes))


def _resident(shape):
    nd = len(shape)
    return pl.BlockSpec(shape, lambda *_: (0,) * nd, pipeline_mode=pl.Buffered(1))


def _nbytes(shape, dtype):
    n = 1
    for s in shape:
        n *= s
    return n * jnp.dtype(dtype).itemsize


def _layer_norm(y, g, b):
    mu = jnp.mean(y, axis=-1, keepdims=True)
    yc = y - mu
    var = jnp.mean(yc * yc, axis=-1, keepdims=True)
    return yc * lax.rsqrt(var + LN_EPS) * g + b


def _sigmoid(z):
    return 1.0 / (1.0 + jnp.exp(-z))


def _softplus(z):
    return jnp.maximum(z, 0.0) + jnp.log(1.0 + jnp.exp(-jnp.abs(z)))


def _split_hi_lo(x):
    hi = x.astype(BF16)
    lo = (x - hi.astype(F32)).astype(BF16)
    return hi, lo


_NT = (((1,), (1,)), ((), ()))
_TN = (((0,), (0,)), ((), ()))


def _gconv_kernel(kind, step_mode, tm, fdim, tiles_per_seq, *refs):
    cw = MXU_COLS
    if step_mode:
        x_ref, w1_ref, wc_ref, w2_ref, g_ref, b_ref, p0_ref, p1_ref, y_ref, st_ref = refs
    else:
        x_ref, w1_ref, wc_ref, w2_ref, g_ref, b_ref, y_ref, st_ref, carry_ref, ubuf_ref = refs

        @pl.when(pl.program_id(0) % tiles_per_seq == 0)
        def _():
            carry_ref[...] = jnp.zeros_like(carry_ref)

    x = x_ref[...]
    xb = x.astype(BF16)
    acc = jnp.zeros((tm, D_MODEL), F32)
    for c in range(fdim // cw):
        lo = c * cw

        def proj(k, lo=lo):
            return jnp.dot(xb, w1_ref[:, k * fdim + lo:k * fdim + lo + cw], preferred_element_type=F32)

        if kind == "ffn":
            u = proj(0)
            other = proj(1)
        else:
            other = proj(0)
            u = proj(1) * proj(2)
        w0 = wc_ref[0:1, lo:lo + cw]
        w1 = wc_ref[1:2, lo:lo + cw]
        w2 = wc_ref[2:3, lo:lo + cw]
        if step_mode:
            z = w0 * p0_ref[:, lo:lo + cw] + w1 * p1_ref[:, lo:lo + cw] + w2 * u
            st_ref[:, lo:lo + cw] = u
        else:
            ubuf_ref[0:8, :] = carry_ref[:, lo:lo + cw]
            ubuf_ref[8:tm + 8, :] = u
            z = w0 * ubuf_ref[6:tm + 6, :] + w1 * ubuf_ref[7:tm + 7, :] + w2 * u
            tail = u[tm - 8:tm, :]
            carry_ref[:, lo:lo + cw] = tail
            st_ref[0, :, lo:lo + cw] = tail
        if kind == "ffn":
            act = z * _sigmoid(z) * other
        else:
            act = other * z
        acc = acc + jnp.dot(act.astype(BF16), w2_ref[lo:lo + cw, :], preferred_element_type=F32)
    y_ref[...] = _layer_norm(DN_ALPHA * x + acc, g_ref[...], b_ref[...])


def _gconv(kind, x, w1, wc, w2, g, b, *, seq_len=None, prev=None, tm=512):
    m = x.shape[0]
    fdim = wc.shape[1]
    nw = w1.shape[1]
    step_mode = prev is not None
    g2, b2 = g.reshape(1, D_MODEL), b.reshape(1, D_MODEL)
    common = [_resident((D_MODEL, nw)), _resident((3, fdim)), _resident((fdim, D_MODEL)),
              _resident((1, D_MODEL)), _resident((1, D_MODEL))]
    wbytes = _nbytes((D_MODEL, nw), BF16) + _nbytes((fdim, D_MODEL), BF16)
    if step_mode:
        tm = m
        kern = functools.partial(_gconv_kernel, kind, True, tm, fdim, 1)
        return pl.pallas_call(
            kern, grid=(1,),
            in_specs=[pl.BlockSpec((tm, D_MODEL), lambda i: (0, 0))] + common
            + [pl.BlockSpec((tm, fdim), lambda i: (0, 0))] * 2,
            out_specs=[pl.BlockSpec((tm, D_MODEL), lambda i: (0, 0)), pl.BlockSpec((tm, fdim), lambda i: (0, 0))],
            out_shape=[jax.ShapeDtypeStruct((m, D_MODEL), F32), jax.ShapeDtypeStruct((m, fdim), F32)],
            compiler_params=_params(1, wbytes + 8 * _nbytes((tm, fdim), F32)),
            name=f"gconv_{kind}_step",
        )(x, w1, wc, w2, g2, b2, prev[0], prev[1])
    tiles_per_seq = seq_len // tm
    n_seq = m // seq_len
    kern = functools.partial(_gconv_kernel, kind, False, tm, fdim, tiles_per_seq)
    return pl.pallas_call(
        kern, grid=(m // tm,),
        in_specs=[pl.BlockSpec((tm, D_MODEL), lambda i: (i, 0))] + common,
        out_specs=[pl.BlockSpec((tm, D_MODEL), lambda i: (i, 0)),
                   pl.BlockSpec((1, 8, fdim), lambda i: (i // tiles_per_seq, 0, 0))],
        out_shape=[jax.ShapeDtypeStruct((m, D_MODEL), F32), jax.ShapeDtypeStruct((n_seq, 8, fdim), F32)],
        scratch_shapes=[pltpu.VMEM((8, fdim), F32), pltpu.VMEM((tm + 8, MXU_COLS), F32)],
        compiler_params=_params(1, wbytes + 6 * _nbytes((tm, D_MODEL), F32) + 12 * _nbytes((tm, MXU_COLS), F32)),
        name=f"gconv_{kind}_seq",
    )(x, w1, wc, w2, g2, b2)


def _proj_kernel(n_cols, outs, parts, x_ref, w_ref, *out_refs):
    cw = MXU_COLS
    if parts == 1:
        xb = x_ref[...].astype(BF16)
    else:
        xb = jnp.concatenate([x_ref[:, j * D_MODEL:(j + 1) * D_MODEL].astype(BF16) for j in range(parts)], axis=0)
    for c in range(n_cols // cw):
        lo = c * cw
        val = None
        for (start, width, scale), o_ref in zip(outs, out_refs):
            if start <= lo < start + width:
                if val is None:
                    val = jnp.dot(xb, w_ref[:, lo:lo + cw], preferred_element_type=F32)
                v = val if scale == 1.0 else val * scale
                o_ref[:, lo - start:lo - start + cw] = v.astype(o_ref.dtype)


def _proj(x, w, outs, *, tm=512, name="proj", row_blocks=None, x_map=None, parts=1):
    if x_map is None:
        tm = min(tm, x.shape[0])
        row_blocks, x_map = x.shape[0] // tm, (lambda i: (i, 0))
    m = row_blocks * tm
    n = w.shape[1]
    spec = tuple((s, wd, sc) for s, wd, sc, _ in outs)
    kern = functools.partial(_proj_kernel, n, spec, parts)
    obytes = sum(_nbytes((tm, wd), dt) for _, wd, _, dt in outs)
    return pl.pallas_call(
        kern, grid=(row_blocks,),
        in_specs=[pl.BlockSpec((tm // parts, parts * D_MODEL), x_map), _resident((D_MODEL, n))],
        out_specs=[pl.BlockSpec((tm, wd), lambda i: (i, 0)) for _, wd, _, _ in outs],
        out_shape=[jax.ShapeDtypeStruct((m, wd), dt) for _, wd, _, dt in outs],
        compiler_params=_params(1, _nbytes((D_MODEL, n), BF16) + 2 * obytes + 4 * _nbytes((tm, D_MODEL), F32)),
        name=name,
    )(x, w)


def _outln_kernel(a_ref, w_ref, x_ref, g_ref, b_ref, y_ref):
    mix = jnp.dot(a_ref[...].astype(BF16), w_ref[...], preferred_element_type=F32)
    y_ref[...] = _layer_norm(DN_ALPHA * x_ref[...] + mix, g_ref[...], b_ref[...])


def _out_ln(a, w, x, g, b, *, tm=512):
    m, kin = a.shape
    tm = min(tm, m)
    return pl.pallas_call(
        _outln_kernel, grid=(m // tm,),
        in_specs=[pl.BlockSpec((tm, kin), lambda i: (i, 0)), _resident((kin, D_MODEL)),
                  pl.BlockSpec((tm, D_MODEL), lambda i: (i, 0)), _resident((1, D_MODEL)), _resident((1, D_MODEL))],
        out_specs=pl.BlockSpec((tm, D_MODEL), lambda i: (i, 0)),
        out_shape=jax.ShapeDtypeStruct((m, D_MODEL), F32),
        compiler_params=_params(1, _nbytes((kin, D_MODEL), BF16) + 8 * _nbytes((tm, D_MODEL), F32)),
        name="out_ln",
    )(a, w, x, g.reshape(1, D_MODEL), b.reshape(1, D_MODEL))


def _sb_attn_kernel(t, nkb, bias_ref, q_ref, k_ref, v_ref, o_ref, k2_ref, v2_ref, carry_ref, acc_ref):
    hp = pl.program_id(1)
    i = pl.program_id(2)
    hd = SB_HEAD_DIM
    lw = 2 * hd

    @pl.when(i == 0)
    def _():
        lane = lax.broadcasted_iota(jnp.int32, (t, lw), 1)

        def build(j, c):
            r0 = pl.multiple_of(j * t, t)
            for src, dst in ((k_ref, k2_ref), (v_ref, v2_ref)):
                blk = src[pl.ds(r0, t), :]
                zero = jnp.zeros_like(blk)
                dst[j, 0:t, :] = jnp.where(lane < hd, blk, zero)
                dst[j, t:2 * t, :] = jnp.where(lane >= hd, blk, zero)
            return c

        lax.fori_loop(0, nkb, build, 0)

    lane = lax.broadcasted_iota(jnp.int32, (t, 2 * t), 1)
    row = lax.broadcasted_iota(jnp.int32, (t, 2 * t), 0)
    bias = jnp.where(lane < t, bias_ref[2 * hp], bias_ref[2 * hp + 1])
    causal = (lane & (t - 1)) < row
    ri = lax.broadcasted_iota(jnp.int32, (t, t), 0)
    ci = lax.broadcasted_iota(jnp.int32, (t, t), 1)
    suffix_ones = jnp.where(ri >= ci, 1.0, 0.0).astype(BF16)
    q = q_ref[...]
    carry_ref[...] = jnp.zeros_like(carry_ref)
    acc_ref[...] = jnp.zeros_like(acc_ref)

    def scores(j, masked):
        z = lax.dot_general(q, k2_ref[j], _NT, preferred_element_type=F32) + bias
        sp = _softplus(z)
        if masked:
            sp = jnp.where(causal, sp, 0.0)
        spb = sp.astype(BF16)
        c_loc = [jnp.dot(spb[:, h * t:(h + 1) * t], suffix_ones, preferred_element_type=F32) for h in range(2)]
        return z, c_loc

    def accumulate(j, z, c_loc, masked):
        car = carry_ref[...]
        reps = t // lw
        car_full = jnp.concatenate([car[:, :lw]] * reps + [car[:, lw:]] * reps, axis=1)
        w = jnp.exp(z - jnp.concatenate(c_loc, axis=1) - car_full)
        if masked:
            w = jnp.where(causal, w, 0.0)
        acc_ref[...] += jnp.dot(w.astype(BF16), v2_ref[j], preferred_element_type=F32)
        carry_ref[...] = car + jnp.concatenate([jnp.broadcast_to(c[:, 0:1], (t, lw)) for c in c_loc], axis=1)

    z, c_loc = scores(i, True)
    accumulate(i, z, c_loc, True)

    def pair(s, c):
        j = i - 1 - 2 * s
        za, ca = scores(j, False)
        zb, cb = scores(j - 1, False)
        accumulate(j, za, ca, False)
        accumulate(j - 1, zb, cb, False)
        return c

    lax.fori_loop(0, i // 2, pair, 0)

    @pl.when(i % 2 == 1)
    def _():
        z, c_loc = scores(0, False)
        accumulate(0, z, c_loc, False)

    o_ref[...] = acc_ref[...].astype(o_ref.dtype)


def _sb_attention_prompt(qb, kb, vb, bias, n_seq, seq_len):
    t = 256
    nq = seq_len // t
    lw = 2 * SB_HEAD_DIM
    kern = functools.partial(_sb_attn_kernel, t, nq)
    scratch = 2 * _nbytes((seq_len * 2, lw), BF16) + _nbytes((t, 2 * lw), F32) + _nbytes((t, lw), F32)
    return pl.pallas_call(
        kern, grid=(n_seq, SB_HEADS // 2, nq),
        in_specs=[pl.BlockSpec(memory_space=pltpu.SMEM),
                  pl.BlockSpec((t, lw), lambda b, h, i: (b * nq + i, h)),
                  pl.BlockSpec((seq_len, lw), lambda b, h, i: (b, h)),
                  pl.BlockSpec((seq_len, lw), lambda b, h, i: (b, h))],
        out_specs=pl.BlockSpec((t, lw), lambda b, h, i: (b * nq + i, h)),
        out_shape=jax.ShapeDtypeStruct(qb.shape, BF16),
        scratch_shapes=[pltpu.VMEM((nq, 2 * t, lw), BF16), pltpu.VMEM((nq, 2 * t, lw), BF16),
                        pltpu.VMEM((t, 2 * lw), F32), pltpu.VMEM((t, lw), F32)],
        compiler_params=_params(3, scratch + 4 * _nbytes((seq_len, lw), BF16) + (16 << 20)),
        name="sb_attn_prompt",
    )(bias, qb, kb, vb)


def _sb_sample_kernel(n_steps, pt_ref, q_ref, bias_ref, ka_ref, kb_ref, va_ref, vb_ref, o_ref, acc_ref, carry_ref):
    p = pl.program_id(1)
    nh = SB_HEADS
    hd = SB_HEAD_DIM
    n = PAGE_SIZE * nh
    lanes = 128
    nblk = n // lanes

    @pl.when(p == 0)
    def _():
        acc_ref[...] = jnp.zeros_like(acc_ref)
        carry_ref[...] = jnp.zeros_like(carry_ref)

    q2 = (q_ref[0] * (hd ** -0.5)).astype(BF16)
    own = (lax.broadcasted_iota(jnp.int32, (nh, n), 1) & (nh - 1)) == lax.broadcasted_iota(jnp.int32, (nh, n), 0)
    ri = lax.broadcasted_iota(jnp.int32, (2 * lanes, lanes), 0) & (lanes - 1)
    ci = lax.broadcasted_iota(jnp.int32, (2 * lanes, lanes), 1)
    suffix_ones2 = jnp.where(ri >= ci, 1.0, 0.0).astype(BF16)
    bias = bias_ref[...]

    def stack(x):
        return jnp.concatenate([x[:, c * lanes:(c + 1) * lanes] for c in range(nblk)], axis=0)

    for k_ref, v_ref in ((ka_ref, va_ref), (kb_ref, vb_ref)):
        kf = k_ref[0].reshape(n, hd).astype(BF16)
        z = lax.dot_general(q2, kf, _NT, preferred_element_type=F32) + bias
        sp = jnp.where(own, _softplus(z), 0.0)
        hi, lo = _split_hi_lo(sp)
        c_st = jnp.dot(jnp.concatenate([stack(hi), stack(lo)], axis=1), suffix_ones2, preferred_element_type=F32)
        blks = [c_st[c * nh:(c + 1) * nh] for c in range(nblk)]
        tots = [jnp.broadcast_to(blk[:, 0:1], (nh, lanes)) for blk in blks]
        run = carry_ref[...]
        parts = [None] * nblk
        for c in reversed(range(nblk)):
            parts[c] = blks[c] + run
            run = run + tots[c]
        carry_ref[...] = run
        w = jnp.where(own, jnp.exp(z - jnp.concatenate(parts, axis=1)), 0.0).astype(BF16)
        vf = v_ref[0].reshape(n, hd).astype(BF16)
        acc_ref[...] += jnp.dot(w, vf, preferred_element_type=F32)

    @pl.when(p == n_steps - 1)
    def _():
        o_ref[0] = acc_ref[...]


def _sb_attention_sample(q, cache_k, cache_v, page_table, bias):
    bsz, n_pages = page_table.shape
    n_steps = n_pages // 2
    nh, hd = SB_HEADS, SB_HEAD_DIM

    def page_spec(off):
        return pl.BlockSpec((1, PAGE_SIZE, nh, hd), lambda b, p, pt: (pt[b, n_pages - 1 - off - 2 * p], 0, 0, 0))

    grid_spec = pltpu.PrefetchScalarGridSpec(
        num_scalar_prefetch=1, grid=(bsz, n_steps),
        in_specs=[pl.BlockSpec((1, nh, hd), lambda b, p, pt: (b, 0, 0)),
                  pl.BlockSpec((nh, 1), lambda b, p, pt: (0, 0)),
                  page_spec(0), page_spec(1), page_spec(0), page_spec(1)],
        out_specs=pl.BlockSpec((1, nh, hd), lambda b, p, pt: (b, 0, 0)),
        scratch_shapes=[pltpu.VMEM((nh, hd), F32), pltpu.VMEM((nh, 128), F32)])
    out = pl.pallas_call(
        functools.partial(_sb_sample_kernel, n_steps), grid_spec=grid_spec,
        out_shape=jax.ShapeDtypeStruct((bsz, nh, hd), F32),
        compiler_params=_params(2, 16 * _nbytes((PAGE_SIZE, nh, 128), F32)),
        name="sb_attn_sample",
    )(page_table, q.reshape(bsz, nh, hd), bias.reshape(nh, 1), cache_k, cache_k, cache_v, cache_v)
    return out.reshape(bsz, nh * hd)


def _gla_in_kernel(x_ref, w_ref, wg_ref, bg_ref, q_ref, k_ref, v_ref, r_ref, la_ref):
    cw = MXU_COLS
    xb = x_ref[...].astype(BF16)

    def cols(lo, n):
        return jnp.dot(xb, w_ref[:, lo:lo + n], preferred_element_type=F32)

    for c in range(GLA_QK // cw):
        q_ref[:, c * cw:(c + 1) * cw] = cols(c * cw, cw) * (GLA_DK ** -0.5)
        k_ref[:, c * cw:(c + 1) * cw] = cols(GLA_QK + c * cw, cw)
    for c in range(GLA_V // cw):
        v_ref[:, c * cw:(c + 1) * cw] = cols(2 * GLA_QK + c * cw, cw)
        r_ref[:, c * cw:(c + 1) * cw] = cols(2 * GLA_QK + GLA_V + c * cw, cw)
    g_low = cols(2 * GLA_QK + 2 * GLA_V, 128).astype(BF16)
    u = jnp.dot(g_low, wg_ref[...], preferred_element_type=F32) + bg_ref[...]
    la_ref[...] = -_softplus(-u) * (1.0 / GLA_TAU)


def _gla_in(x, w_pad, wg_pad, bg, *, tm=512):
    m = x.shape[0]
    tm = min(tm, m)
    n = w_pad.shape[1]
    widths = (GLA_QK, GLA_QK, GLA_V, GLA_V, GLA_QK)
    return pl.pallas_call(
        _gla_in_kernel, grid=(m // tm,),
        in_specs=[pl.BlockSpec((tm, D_MODEL), lambda i: (i, 0)), _resident((D_MODEL, n)),
                  _resident((128, GLA_QK)), _resident((1, GLA_QK))],
        out_specs=[pl.BlockSpec((tm, wd), lambda i: (i, 0)) for wd in widths],
        out_shape=[jax.ShapeDtypeStruct((m, wd), F32) for wd in widths],
        compiler_params=_params(1, _nbytes((D_MODEL, n), BF16) + 12 * _nbytes((tm, D_MODEL), F32)),
        name="gla_in",
    )(x, w_pad, wg_pad, bg.reshape(1, GLA_QK))


def _gla_kernel(t_rows, n_t, q_ref, k_ref, la_ref, v_ref, r_ref, gn_ref, o_ref, s_ref, st_ref):
    t = pl.program_id(2)
    ch = GLA_CHUNK

    @pl.when(t == 0)
    def _():
        st_ref[...] = jnp.zeros_like(st_ref)

    ri = lax.broadcasted_iota(jnp.int32, (ch, ch), 0)
    ci = lax.broadcasted_iota(jnp.int32, (ch, ch), 1)
    tri = ci <= ri
    prefix_ones = jnp.where(tri, 1.0, 0.0).astype(BF16)
    prefix_ones2 = jnp.concatenate([prefix_ones, prefix_ones], axis=1)
    gn = gn_ref[...]
    for c in range(t_rows // ch):
        sl = slice(c * ch, (c + 1) * ch)
        hi, lo = _split_hi_lo(la_ref[sl, :])
        b = jnp.dot(prefix_ones2, jnp.concatenate([hi, lo], axis=0), preferred_element_type=F32)
        mid = b[ch // 2 - 1:ch // 2, :]
        last = b[ch - 1:ch, :]
        qs = q_ref[sl, :] * jnp.exp(b - mid)
        ks = k_ref[sl, :] * jnp.exp(mid - b)
        scores = lax.dot_general(qs.astype(BF16), ks.astype(BF16), _NT, preferred_element_type=F32)
        scores = jnp.where(tri, scores, 0.0)
        vb = v_ref[sl, :].astype(BF16)
        st = st_ref[...]
        qe = (qs * jnp.exp(mid)).astype(BF16)
        o = (jnp.dot(scores.astype(BF16), vb, preferred_element_type=F32)
             + lax.dot_general(qe, st.astype(BF16), _NT, preferred_element_type=F32))
        kd = (ks * jnp.exp(last - mid)).astype(BF16)
        st_ref[...] = jnp.exp(last) * st + lax.dot_general(vb, kd, _TN, preferred_element_type=F32)
        o = o * lax.rsqrt(jnp.mean(o * o, axis=-1, keepdims=True) + LN_EPS) * gn
        r = r_ref[sl, :]
        o_ref[sl, :] = (o * (r * _sigmoid(r))).astype(o_ref.dtype)

    @pl.when(t == n_t - 1)
    def _():
        s_ref[0, 0] = st_ref[...].T


def _gla_prompt(q, k, la, v, r, g_norm, n_seq, seq_len, *, t_rows=512):
    n_t = seq_len // t_rows
    kern = functools.partial(_gla_kernel, t_rows, n_t)
    qk_spec = pl.BlockSpec((t_rows, GLA_DK), lambda b, h, t: (b * n_t + t, h))
    v_spec = pl.BlockSpec((t_rows, GLA_DV), lambda b, h, t: (b * n_t + t, h))
    return pl.pallas_call(
        kern, grid=(n_seq, GLA_HEADS, n_t),
        in_specs=[qk_spec, qk_spec, qk_spec, v_spec, v_spec, pl.BlockSpec((1, GLA_DV), lambda b, h, t: (0, 0))],
        out_specs=[v_spec, pl.BlockSpec((1, 1, GLA_DK, GLA_DV), lambda b, h, t: (b, h, 0, 0))],
        out_shape=[jax.ShapeDtypeStruct((n_seq * seq_len, GLA_V), BF16),
                   jax.ShapeDtypeStruct((n_seq, GLA_HEADS, GLA_DK, GLA_DV), F32)],
        scratch_shapes=[pltpu.VMEM((GLA_DV, GLA_DK), F32)],
        compiler_params=_params(3, 16 * _nbytes((t_rows, GLA_DV), F32)),
        name="gla_prompt",
    )(q, k, la, v, r, g_norm.reshape(1, GLA_DV))


def _gla_sample_kernel(q_ref, k_ref, la_ref, v_ref, r_ref, gn_ref, s0_ref, o_ref, s_ref):
    gn = gn_ref[...]
    for h in range(GLA_HEADS):
        a = jnp.exp(la_ref[0, h])
        vrow = v_ref[0, :, h * GLA_DV:(h + 1) * GLA_DV]
        s_new = a * s0_ref[0, h] + k_ref[0, h] * vrow
        s_ref[0, h] = s_new
        o = jnp.sum(q_ref[0, h] * s_new, axis=0, keepdims=True)
        o = o * lax.rsqrt(jnp.mean(o * o, axis=-1, keepdims=True) + LN_EPS) * gn
        r = r_ref[0, :, h * GLA_DV:(h + 1) * GLA_DV]
        o_ref[0, :, h * GLA_DV:(h + 1) * GLA_DV] = o * (r * _sigmoid(r))


def _gla_sample(q, k, la, v, r, g_norm, s0):
    bsz = q.shape[0]
    col = lambda t: t.reshape(bsz, GLA_HEADS, GLA_DK, 1)
    col_spec = pl.BlockSpec((1, GLA_HEADS, GLA_DK, 1), lambda b: (b, 0, 0, 0))
    row_spec = pl.BlockSpec((1, 1, GLA_V), lambda b: (b, 0, 0))
    st_spec = pl.BlockSpec((1, GLA_HEADS, GLA_DK, GLA_DV), lambda b: (b, 0, 0, 0))
    o, s = pl.pallas_call(
        _gla_sample_kernel, grid=(bsz,),
        in_specs=[col_spec, col_spec, col_spec, row_spec, row_spec,
                  pl.BlockSpec((1, GLA_DV), lambda b: (0, 0)), st_spec],
        out_specs=[row_spec, st_spec],
        out_shape=[jax.ShapeDtypeStruct((bsz, 1, GLA_V), F32), jax.ShapeDtypeStruct(s0.shape, F32)],
        compiler_params=_params(1, 8 * _nbytes((GLA_HEADS, GLA_DK, GLA_DV), F32)),
        name="gla_sample",
    )(col(q), col(k), col(la), v.reshape(bsz, 1, GLA_V), r.reshape(bsz, 1, GLA_V), g_norm.reshape(1, GLA_DV), s0)
    return o.reshape(bsz, GLA_V), s


def _dsw_attn_kernel(d, bpc, slope_ref, q_ref, kp_ref, kc_ref, vp_ref, vc_ref, o_ref, lse_ref):
    t = pl.program_id(1)
    sb = pl.program_id(2)
    qb = DSW_QB
    hd = DSW_HEAD_DIM
    has_prev = jnp.minimum(t + (sb & (bpc - 1)), 1)
    row = lax.broadcasted_iota(jnp.int32, (qb, 2 * qb), 0)
    col = lax.broadcasted_iota(jnp.int32, (qb, 2 * qb), 1)
    steps_back = row + qb - col
    first_col = qb - has_prev * qb
    valid = (steps_back >= 0) & (steps_back <= qb) & (col >= first_col)
    dist = steps_back.astype(F32) * float(d)
    q = q_ref[...]
    kk = jnp.concatenate([kp_ref[...], kc_ref[...]], axis=0)
    vv = jnp.concatenate([vp_ref[...], vc_ref[...]], axis=0)
    for h in range(DSW_HEADS):
        hs = slice(h * hd, (h + 1) * hd)
        s = lax.dot_general(q[:, hs], kk[:, hs], _NT, preferred_element_type=F32) - slope_ref[h] * dist
        s = jnp.where(valid, s, -jnp.inf)
        m = jnp.max(s, axis=-1, keepdims=True)
        p = jnp.exp(s - m)
        l = jnp.sum(p, axis=-1, keepdims=True)
        o = jnp.dot(p.astype(BF16), vv[:, hs], preferred_element_type=F32) / l
        o_ref[:, hs] = o
        lse_ref[:, hs] = jnp.broadcast_to(m + jnp.log(l), (qb, hd))


def _dsw_attention_prompt(d, qc, kc, vc, slopes, n_seq, seq_len):
    qb = DSW_QB
    n_tiles = seq_len // DSW_TILE
    nsb = DSW_TILE // qb
    bpc = nsb // d

    def cur(b, t, sb):
        return ((b * n_tiles + t) * nsb + sb, 0)

    def prev(b, t, sb):
        first = (sb & (bpc - 1)) == 0
        back = jnp.where(first, jnp.where(t > 0, nsb - bpc + 1, 0), 1)
        return ((b * n_tiles + t) * nsb + sb - back, 0)

    def nat(b, t, sb):
        return ((b * n_tiles + t) * bpc + (sb & (bpc - 1)), sb // bpc)

    blk = lambda im: pl.BlockSpec((qb, DSW_INNER), im)
    m = n_seq * seq_len
    o, lse = pl.pallas_call(
        functools.partial(_dsw_attn_kernel, d, bpc), grid=(n_seq, n_tiles, nsb),
        in_specs=[pl.BlockSpec(memory_space=pltpu.SMEM), blk(cur), blk(prev), blk(cur), blk(prev), blk(cur)],
        out_specs=[blk(nat), blk(nat)],
        out_shape=[jax.ShapeDtypeStruct((m // d, d * DSW_INNER), F32)] * 2,
        compiler_params=_params(3, 32 * _nbytes((qb, DSW_INNER), F32)),
        name=f"dsw_attn_prompt_d{d}",
    )(slopes, qc, kc, kc, vc, vc)
    return o.reshape(m, DSW_INNER), lse.reshape(m, DSW_INNER)


def _dsw_outln_kernel(o0_ref, o1_ref, o2_ref, l0_ref, l1_ref, l2_ref, w_ref, x_ref, g_ref, b_ref, y_ref):
    l0, l1, l2 = l0_ref[...], l1_ref[...], l2_ref[...]
    m = jnp.maximum(jnp.maximum(l0, l1), l2)
    e0, e1, e2 = jnp.exp(l0 - m), jnp.exp(l1 - m), jnp.exp(l2 - m)
    o = (o0_ref[...] * e0 + o1_ref[...] * e1 + o2_ref[...] * e2) / (e0 + e1 + e2)
    mix = jnp.dot(o.astype(BF16), w_ref[...], preferred_element_type=F32)
    y_ref[...] = _layer_norm(DN_ALPHA * x_ref[...] + mix, g_ref[...], b_ref[...])


def _dsw_out_ln(os_, lses, w, x, g, b, *, tm=512):
    m = x.shape[0]
    part = pl.BlockSpec((tm, DSW_INNER), lambda i: (i, 0))
    row = pl.BlockSpec((tm, D_MODEL), lambda i: (i, 0))
    return pl.pallas_call(
        _dsw_outln_kernel, grid=(m // tm,),
        in_specs=[part] * 6 + [_resident((DSW_INNER, D_MODEL)), row, _resident((1, D_MODEL)), _resident((1, D_MODEL))],
        out_specs=row,
        out_shape=jax.ShapeDtypeStruct((m, D_MODEL), F32),
        compiler_params=_params(1, 24 * _nbytes((tm, DSW_INNER), F32) + 8 * _nbytes((tm, D_MODEL), F32)),
        name="dsw_out_ln",
    )(*os_, *lses, w, x, g.reshape(1, D_MODEL), b.reshape(1, D_MODEL))


def _dsw_sample_kernel(slope_ref, qkv_ref, r0_ref, r1_ref, r2_ref, o_ref):
    nk = DSW_QB
    scale = DSW_HEAD_DIM ** -0.5
    steps_back = (nk - lax.broadcasted_iota(jnp.int32, (nk, DSW_HEADS, 1), 0)).astype(F32)
    parts = []
    for g, ((_, d), rows_ref) in enumerate(zip(DSW_GROUPS, (r0_ref, r1_ref, r2_ref))):
        q = qkv_ref[0, 3 * g] * scale
        k_new = qkv_ref[0, 3 * g + 1]
        v_new = qkv_ref[0, 3 * g + 2]
        k = rows_ref[0, :, 0, 0]
        v = rows_ref[0, :, 0, 1]
        s = jnp.sum(k * q, axis=-1, keepdims=True) - slope_ref[...][g] * (steps_back * float(d))
        s_new = jnp.sum(q * k_new, axis=-1, keepdims=True)
        m = jnp.maximum(jnp.max(s, axis=0), s_new)
        p = jnp.exp(s - m)
        p_new = jnp.exp(s_new - m)
        l = jnp.sum(p, axis=0) + p_new
        num = jnp.sum(p * v, axis=0) + p_new * v_new
        parts.append((m, l, num))
    m_all = jnp.maximum(jnp.maximum(parts[0][0], parts[1][0]), parts[2][0])
    den = sum(l * jnp.exp(m - m_all) for m, l, _ in parts)
    num = sum(n * jnp.exp(m - m_all) for m, _, n in parts)
    o_ref[0] = num / den


def _dsw_attention_sample(qkv, bufs, slopes):
    bsz = qkv.shape[0]
    nh, hd = DSW_HEADS, DSW_HEAD_DIM
    row_specs, rows = [], []
    for (w, d), buf in zip(DSW_GROUPS, bufs):
        rows.append(buf.reshape(bsz, w // d, d, 2, nh, hd))
        row_specs.append(pl.BlockSpec((1, w // d, 1, 2, nh, hd), lambda b: (b, 0, 0, 0, 0, 0)))
    ng = len(DSW_GROUPS)
    out = pl.pallas_call(
        _dsw_sample_kernel, grid=(bsz,),
        in_specs=[pl.BlockSpec((ng, nh, 1), lambda b: (0, 0, 0)),
                  pl.BlockSpec((1, 3 * ng, nh, hd), lambda b: (b, 0, 0, 0))] + row_specs,
        out_specs=pl.BlockSpec((1, nh, hd), lambda b: (b, 0, 0)),
        out_shape=jax.ShapeDtypeStruct((bsz, nh, hd), F32),
        compiler_params=_params(1, 2 * ng * _nbytes((DSW_QB, 2, nh, 128), F32)),
        name="dsw_attn_sample",
    )(slopes, qkv, *rows)
    return out.reshape(bsz, nh * hd)


def _shift_kernel(t_rows, n_t, cur_ref, new_ref, o_ref):
    last = pl.program_id(1) == n_t - 1

    @pl.when(jnp.logical_not(last))
    def _():
        o_ref[...] = cur_ref[...]

    @pl.when(last)
    def _():
        o_ref[0, 0:t_rows - 1] = cur_ref[0, 1:t_rows]
        o_ref[0, t_rows - 1] = new_ref[0]


def _shift_append(buf, new_rows):
    bsz, w = buf.shape[:2]
    tail = buf.shape[2:]
    t_rows = min(w, 256)
    n_t = w // t_rows
    zeros = (0,) * len(tail)
    return pl.pallas_call(
        functools.partial(_shift_kernel, t_rows, n_t), grid=(bsz, n_t),
        in_specs=[pl.BlockSpec(tuple(pl.Element(s) for s in (1, t_rows) + tail),
                               lambda b, t: (b, jnp.where(t == n_t - 1, t * t_rows, t * t_rows + 1)) + zeros),
                  pl.BlockSpec((1,) + tail, lambda b, t: (b,) + zeros)],
        out_specs=pl.BlockSpec((1, t_rows) + tail, lambda b, t: (b, t) + zeros),
        out_shape=jax.ShapeDtypeStruct(buf.shape, buf.dtype),
        compiler_params=_params(2, 4 * _nbytes((t_rows, 2, DSW_HEADS, 128), buf.dtype)),
        name="shift_append",
    )(buf, new_rows)


def kernel(x_prompt, x_sample, cache_sc_conv, cache_sb_k, cache_sb_v, state_gla, cache_dsw_kv0, cache_dsw_kv1,
           cache_dsw_kv2, state_ffn_conv, page_table, ln_g, ln_b, w_sc_in, w_sc_conv, w_sc_out, w_sb_qkv, w_sb_out,
           b_sb, w_gla_in, w_gla_gate_up, b_gla_gate, g_gla_norm, w_gla_out, w_dsw_qkv, w_dsw_out, w_ffn_up,
           w_ffn_conv, w_ffn_down):
    n_seq, seq_len, _ = x_prompt.shape
    bsz = x_sample.shape[0]
    xp = x_prompt.reshape(n_seq * seq_len, D_MODEL)
    xs = x_sample.reshape(bsz, D_MODEL)
    bf = lambda w: w.astype(BF16)
    ffn_p, ffn_s = [], []

    def ffn(layer, xp, xs):
        w_up, w_down = bf(w_ffn_up[layer]), bf(w_ffn_down[layer])
        g, b = ln_g[layer, 1], ln_b[layer, 1]
        xp, st_p = _gconv("ffn", xp, w_up, w_ffn_conv[layer], w_down, g, b, seq_len=seq_len)
        prev = state_ffn_conv[layer]
        xs, u_s = _gconv("ffn", xs, w_up, w_ffn_conv[layer], w_down, g, b, prev=(prev[:, 0], prev[:, 1]))
        ffn_p.append(st_p[:, 6:8])
        ffn_s.append(jnp.stack([prev[:, 1], u_s], axis=1))
        return xp, xs

    w_in, w_out = bf(w_sc_in), bf(w_sc_out)
    xp, sc_st = _gconv("sc", xp, w_in, w_sc_conv, w_out, ln_g[0, 0], ln_b[0, 0], seq_len=seq_len)
    xs, sc_u = _gconv("sc", xs, w_in, w_sc_conv, w_out, ln_g[0, 0], ln_b[0, 0],
                      prev=(cache_sc_conv[:, 0], cache_sc_conv[:, 1]))
    sc_p = sc_st[:, 6:8]
    sc_s = jnp.stack([cache_sc_conv[:, 1], sc_u], axis=1)
    xp, xs = ffn(0, xp, xs)

    w_qkv, w_out = bf(w_sb_qkv), bf(w_sb_out)
    width = SB_HEADS * SB_HEAD_DIM
    scale = SB_HEAD_DIM ** -0.5
    qb, kf, vf, kb, vb = _proj(xp, w_qkv, [(0, width, scale, BF16), (width, width, 1.0, F32),
                                           (2 * width, width, 1.0, F32), (width, width, 1.0, BF16),
                                           (2 * width, width, 1.0, BF16)], name="sb_qkv_prompt")
    o = _sb_attention_prompt(qb, kb, vb, b_sb, n_seq, seq_len)
    sb_k_p = kf.reshape(n_seq, seq_len, SB_HEADS, SB_HEAD_DIM)
    sb_v_p = vf.reshape(n_seq, seq_len, SB_HEADS, SB_HEAD_DIM)
    xp = _out_ln(o, w_out, xp, ln_g[1, 0], ln_b[1, 0])
    qs, ks, vs = _proj(xs, w_qkv, [(0, width, 1.0, F32), (width, width, 1.0, F32), (2 * width, width, 1.0, F32)],
                       name="sb_qkv_sample")
    o = _sb_attention_sample(qs, cache_sb_k, cache_sb_v, page_table, b_sb)
    sb_k_s = ks.reshape(bsz, 1, SB_HEADS, SB_HEAD_DIM)
    sb_v_s = vs.reshape(bsz, 1, SB_HEADS, SB_HEAD_DIM)
    xs = _out_ln(o, w_out, xs, ln_g[1, 0], ln_b[1, 0])
    xp, xs = ffn(1, xp, xs)

    n_in = w_gla_in.shape[1]
    w_in = bf(jnp.pad(w_gla_in, ((0, 0), (0, 2 * GLA_QK + 2 * GLA_V + 128 - n_in))))
    w_gate = bf(jnp.pad(w_gla_gate_up, ((0, 128 - GLA_RANK), (0, 0))))
    w_out = bf(w_gla_out)
    q, k, v, r, la = _gla_in(xp, w_in, w_gate, b_gla_gate)
    o, gla_p = _gla_prompt(q, k, la, v, r, g_gla_norm, n_seq, seq_len)
    xp = _out_ln(o, w_out, xp, ln_g[2, 0], ln_b[2, 0])
    q, k, v, r, la = _gla_in(xs, w_in, w_gate, b_gla_gate)
    o, gla_s = _gla_sample(q, k, la, v, r, g_gla_norm, state_gla)
    xs = _out_ln(o, w_out, xs, ln_g[2, 0], ln_b[2, 0])
    xp, xs = ffn(2, xp, xs)

    w_qkv, w_out = bf(w_dsw_qkv), bf(w_dsw_out)
    ng = len(DSW_GROUPS)
    n_heads = ng * DSW_HEADS
    slopes = (2.0 ** (-8.0 * jnp.arange(1, n_heads + 1, dtype=F32) / n_heads)).reshape(ng, DSW_HEADS)
    m = n_seq * seq_len
    tm = 512
    os_, lses = [], []
    for g, (_, d) in enumerate(DSW_GROUPS):
        rows_per_class = DSW_TILE // d
        parts = max(1, tm // rows_per_class)
        steps_per_tile = DSW_TILE // tm
        x_map = (lambda i: (i, 0)) if d == 1 else (lambda i: (i // steps_per_tile, i % steps_per_tile))
        base = g * 3 * DSW_INNER
        qc, kc, vc = _proj(xp.reshape(m // d, d * D_MODEL), w_qkv,
                           [(base, DSW_INNER, DSW_HEAD_DIM ** -0.5, BF16), (base + DSW_INNER, DSW_INNER, 1.0, BF16),
                            (base + 2 * DSW_INNER, DSW_INNER, 1.0, BF16)],
                           tm=tm, name=f"dsw_qkv_prompt_d{d}", row_blocks=m // tm, x_map=x_map, parts=parts)
        o, lse = _dsw_attention_prompt(d, qc, kc, vc, slopes[g], n_seq, seq_len)
        os_.append(o)
        lses.append(lse)
    w_max = max(w for w, _ in DSW_GROUPS)
    per_seq, tail = seq_len // tm, w_max // tm
    kv_tail = _proj(xp, w_qkv, [(g * 3 * DSW_INNER + DSW_INNER, 2 * DSW_INNER, 1.0, F32) for g in range(ng)],
                    tm=tm, name="dsw_kv_tail", row_blocks=n_seq * tail,
                    x_map=lambda i: ((i // tail) * per_seq + per_seq - tail + i % tail, 0))
    dsw_p = [kv.reshape(n_seq, w_max, 2, DSW_HEADS, DSW_HEAD_DIM)[:, w_max - w:]
             for kv, (w, _) in zip(kv_tail, DSW_GROUPS)]
    xp = _dsw_out_ln(os_, lses, w_out, xp, ln_g[3, 0], ln_b[3, 0])
    bufs = [cache_dsw_kv0, cache_dsw_kv1, cache_dsw_kv2]
    (qkv,) = _proj(xs, w_qkv, [(0, w_qkv.shape[1], 1.0, F32)], name="dsw_qkv_sample")
    qkv = qkv.reshape(bsz, 3 * ng, DSW_HEADS, DSW_HEAD_DIM)
    o = _dsw_attention_sample(qkv, bufs, slopes.reshape(ng, DSW_HEADS, 1))
    xs = _out_ln(o, w_out, xs, ln_g[3, 0], ln_b[3, 0])
    dsw_s = [_shift_append(bufs[g], qkv[:, 3 * g + 1:3 * g + 3]) for g in range(ng)]
    xp, xs = ffn(3, xp, xs)

    return (xp.reshape(n_seq, seq_len, D_MODEL), xs.reshape(bsz, 1, D_MODEL), sc_p, sc_s, sb_k_p, sb_v_p, sb_k_s,
            sb_v_s, gla_p, gla_s, dsw_p[0], dsw_p[1], dsw_p[2], dsw_s[0], dsw_s[1], dsw_s[2],
            jnp.stack(ffn_p), jnp.stack(ffn_s))
```

```python
import functools

import jax
import jax.numpy as jnp
from jax import lax
from jax.experimental import pallas as pl
from jax.experimental.pallas import tpu as pltpu

F32 = jnp.float32
BF16 = jnp.bfloat16

D_MODEL = 1024
DEPTH = 4
DN_ALPHA = (2.0 * DEPTH) ** 0.25
LN_EPS = 1e-5
PAGE_SIZE = 128

SB_HEADS = 16
SB_HEAD_DIM = 64

GLA_HEADS = 4
GLA_DK = 128
GLA_DV = 256
GLA_RANK = 16
GLA_TAU = 16.0
GLA_QK = GLA_HEADS * GLA_DK
GLA_V = GLA_HEADS * GLA_DV
GLA_CHUNK = 128

DSW_GROUPS = ((128, 1), (512, 4), (2048, 16))
DSW_HEADS = 8
DSW_HEAD_DIM = 64
DSW_INNER = DSW_HEADS * DSW_HEAD_DIM
DSW_TILE = 2048
DSW_QB = 128

D_FF = 2816
MXU_COLS = 256
V7X_VMEM_BYTES = 64 * 1024 * 1024
VMEM_CAP = V7X_VMEM_BYTES - 6 * 1024 * 1024


def _vmem_limit(block_bytes):
    return int(min(VMEM_CAP, block_bytes * 5 // 4 + (8 << 20)))


def _params(n_grid, block_bytes):
    return pltpu.CompilerParams(dimension_semantics=("arbitrary",) * n_grid,
                                vmem_limit_bytes=_vmem_limit(block_bytes))


def _resident(shape):
    nd = len(shape)
    return pl.BlockSpec(shape, lambda *_: (0,) * nd, pipeline_mode=pl.Buffered(1))


def _nbytes(shape, dtype):
    n = 1
    for s in shape:
        n *= s
    return n * jnp.dtype(dtype).itemsize


def _layer_norm(y, g, b):
    mu = jnp.mean(y, axis=-1, keepdims=True)
    yc = y - mu
    var = jnp.mean(yc * yc, axis=-1, keepdims=True)
    return yc * lax.rsqrt(var + LN_EPS) * g + b


def _sigmoid(z):
    return 1.0 / (1.0 + jnp.exp(-z))


def _softplus(z):
    return jnp.maximum(z, 0.0) + jnp.log(1.0 + jnp.exp(-jnp.abs(z)))


def _split_hi_lo(x):
    hi = x.astype(BF16)
    lo = (x - hi.astype(F32)).astype(BF16)
    return hi, lo


_NT = (((1,), (1,)), ((), ()))
_TN = (((0,), (0,)), ((), ()))


def _gconv_kernel(kind, step_mode, tm, fdim, tiles_per_seq, *refs):
    cw = MXU_COLS
    if step_mode:
        x_ref, w1_ref, wc_ref, w2_ref, g_ref, b_ref, p0_ref, p1_ref, y_ref, st_ref = refs
    else:
        x_ref, w1_ref, wc_ref, w2_ref, g_ref, b_ref, y_ref, st_ref, carry_ref, ubuf_ref = refs

        @pl.when(pl.program_id(0) % tiles_per_seq == 0)
        def _():
            carry_ref[...] = jnp.zeros_like(carry_ref)

    x = x_ref[...]
    xb = x.astype(BF16)
    acc = jnp.zeros((tm, D_MODEL), F32)
    for c in range(fdim // cw):
        lo = c * cw

        def proj(k, lo=lo):
            return jnp.dot(xb, w1_ref[:, k * fdim + lo:k * fdim + lo + cw], preferred_element_type=F32)

        if kind == "ffn":
            u = proj(0)
            other = proj(1)
        else:
            other = proj(0)
            u = proj(1) * proj(2)
        w0 = wc_ref[0:1, lo:lo + cw]
        w1 = wc_ref[1:2, lo:lo + cw]
        w2 = wc_ref[2:3, lo:lo + cw]
        if step_mode:
            z = w0 * p0_ref[:, lo:lo + cw] + w1 * p1_ref[:, lo:lo + cw] + w2 * u
            st_ref[:, lo:lo + cw] = u
        else:
            ubuf_ref[0:8, :] = carry_ref[:, lo:lo + cw]
            ubuf_ref[8:tm + 8, :] = u
            z = w0 * ubuf_ref[6:tm + 6, :] + w1 * ubuf_ref[7:tm + 7, :] + w2 * u
            tail = u[tm - 8:tm, :]
            carry_ref[:, lo:lo + cw] = tail
            st_ref[0, :, lo:lo + cw] = tail
        if kind == "ffn":
            act = z * _sigmoid(z) * other
        else:
            act = other * z
        acc = acc + jnp.dot(act.astype(BF16), w2_ref[lo:lo + cw, :], preferred_element_type=F32)
    y_ref[...] = _layer_norm(DN_ALPHA * x + acc, g_ref[...], b_ref[...])


def _gconv(kind, x, w1, wc, w2, g, b, *, seq_len=None, prev=None, tm=512):
    m = x.shape[0]
    fdim = wc.shape[1]
    nw = w1.shape[1]
    step_mode = prev is not None
    g2, b2 = g.reshape(1, D_MODEL), b.reshape(1, D_MODEL)
    common = [_resident((D_MODEL, nw)), _resident((3, fdim)), _resident((fdim, D_MODEL)),
              _resident((1, D_MODEL)), _resident((1, D_MODEL))]
    wbytes = _nbytes((D_MODEL, nw), BF16) + _nbytes((fdim, D_MODEL), BF16)
    if step_mode:
        tm = m
        kern = functools.partial(_gconv_kernel, kind, True, tm, fdim, 1)
        return pl.pallas_call(
            kern, grid=(1,),
            in_specs=[pl.BlockSpec((tm, D_MODEL), lambda i: (0, 0))] + common
            + [pl.BlockSpec((tm, fdim), lambda i: (0, 0))] * 2,
            out_specs=[pl.BlockSpec((tm, D_MODEL), lambda i: (0, 0)), pl.BlockSpec((tm, fdim), lambda i: (0, 0))],
            out_shape=[jax.ShapeDtypeStruct((m, D_MODEL), F32), jax.ShapeDtypeStruct((m, fdim), F32)],
            compiler_params=_params(1, wbytes + 8 * _nbytes((tm, fdim), F32)),
            name=f"gconv_{kind}_step",
        )(x, w1, wc, w2, g2, b2, prev[0], prev[1])
    tiles_per_seq = seq_len // tm
    n_seq = m // seq_len
    kern = functools.partial(_gconv_kernel, kind, False, tm, fdim, tiles_per_seq)
    return pl.pallas_call(
        kern, grid=(m // tm,),
        in_specs=[pl.BlockSpec((tm, D_MODEL), lambda i: (i, 0))] + common,
        out_specs=[pl.BlockSpec((tm, D_MODEL), lambda i: (i, 0)),
                   pl.BlockSpec((1, 8, fdim), lambda i: (i // tiles_per_seq, 0, 0))],
        out_shape=[jax.ShapeDtypeStruct((m, D_MODEL), F32), jax.ShapeDtypeStruct((n_seq, 8, fdim), F32)],
        scratch_shapes=[pltpu.VMEM((8, fdim), F32), pltpu.VMEM((tm + 8, MXU_COLS), F32)],
        compiler_params=_params(1, wbytes + 6 * _nbytes((tm, D_MODEL), F32) + 12 * _nbytes((tm, MXU_COLS), F32)),
        name=f"gconv_{kind}_seq",
    )(x, w1, wc, w2, g2, b2)


def _proj_kernel(n_cols, outs, parts, x_ref, w_ref, *out_refs):
    cw = MXU_COLS
    if parts == 1:
        xb = x_ref[...].astype(BF16)
    else:
        xb = jnp.concatenate([x_ref[:, j * D_MODEL:(j + 1) * D_MODEL].astype(BF16) for j in range(parts)], axis=0)
    for c in range(n_cols // cw):
        lo = c * cw
        val = None
        for (start, width, scale), o_ref in zip(outs, out_refs):
            if start <= lo < start + width:
                if val is None:
                    val = jnp.dot(xb, w_ref[:, lo:lo + cw], preferred_element_type=F32)
                v = val if scale == 1.0 else val * scale
                o_ref[:, lo - start:lo - start + cw] = v.astype(o_ref.dtype)


def _proj(x, w, outs, *, tm=512, name="proj", row_blocks=None, x_map=None, parts=1):
    if x_map is None:
        tm = min(tm, x.shape[0])
        row_blocks, x_map = x.shape[0] // tm, (lambda i: (i, 0))
    m = row_blocks * tm
    n = w.shape[1]
    spec = tuple((s, wd, sc) for s, wd, sc, _ in outs)
    kern = functools.partial(_proj_kernel, n, spec, parts)
    obytes = sum(_nbytes((tm, wd), dt) for _, wd, _, dt in outs)
    return pl.pallas_call(
        kern, grid=(row_blocks,),
        in_specs=[pl.BlockSpec((tm // parts, parts * D_MODEL), x_map), _resident((D_MODEL, n))],
        out_specs=[pl.BlockSpec((tm, wd), lambda i: (i, 0)) for _, wd, _, _ in outs],
        out_shape=[jax.ShapeDtypeStruct((m, wd), dt) for _, wd, _, dt in outs],
        compiler_params=_params(1, _nbytes((D_MODEL, n), BF16) + 2 * obytes + 4 * _nbytes((tm, D_MODEL), F32)),
        name=name,
    )(x, w)


def _outln_kernel(a_ref, w_ref, x_ref, g_ref, b_ref, y_ref):
    mix = jnp.dot(a_ref[...].astype(BF16), w_ref[...], preferred_element_type=F32)
    y_ref[...] = _layer_norm(DN_ALPHA * x_ref[...] + mix, g_ref[...], b_ref[...])


def _out_ln(a, w, x, g, b, *, tm=512):
    m, kin = a.shape
    tm = min(tm, m)
    return pl.pallas_call(
        _outln_kernel, grid=(m // tm,),
        in_specs=[pl.BlockSpec((tm, kin), lambda i: (i, 0)), _resident((kin, D_MODEL)),
                  pl.BlockSpec((tm, D_MODEL), lambda i: (i, 0)), _resident((1, D_MODEL)), _resident((1, D_MODEL))],
        out_specs=pl.BlockSpec((tm, D_MODEL), lambda i: (i, 0)),
        out_shape=jax.ShapeDtypeStruct((m, D_MODEL), F32),
        compiler_params=_params(1, _nbytes((kin, D_MODEL), BF16) + 8 * _nbytes((tm, D_MODEL), F32)),
        name="out_ln",
    )(a, w, x, g.reshape(1, D_MODEL), b.reshape(1, D_MODEL))


def _sb_attn_kernel(t, nkb, bias_ref, q_ref, k_ref, v_ref, o_ref, k2_ref, v2_ref, carry_ref, acc_ref):
    hp = pl.program_id(1)
    i = pl.program_id(2)
    hd = SB_HEAD_DIM
    lw = 2 * hd

    @pl.when(i == 0)
    def _():
        lane = lax.broadcasted_iota(jnp.int32, (t, lw), 1)

        def build(j, c):
            r0 = pl.multiple_of(j * t, t)
            for src, dst in ((k_ref, k2_ref), (v_ref, v2_ref)):
                blk = src[pl.ds(r0, t), :]
                zero = jnp.zeros_like(blk)
                dst[j, 0:t, :] = jnp.where(lane < hd, blk, zero)
                dst[j, t:2 * t, :] = jnp.where(lane >= hd, blk, zero)
            return c

        lax.fori_loop(0, nkb, build, 0)

    lane = lax.broadcasted_iota(jnp.int32, (t, 2 * t), 1)
    row = lax.broadcasted_iota(jnp.int32, (t, 2 * t), 0)
    bias = jnp.where(lane < t, bias_ref[2 * hp], bias_ref[2 * hp + 1])
    causal = (lane & (t - 1)) < row
    ri = lax.broadcasted_iota(jnp.int32, (t, t), 0)
    ci = lax.broadcasted_iota(jnp.int32, (t, t), 1)
    suffix_ones = jnp.where(ri >= ci, 1.0, 0.0).astype(BF16)
    q = q_ref[...]
    carry_ref[...] = jnp.zeros_like(carry_ref)
    acc_ref[...] = jnp.zeros_like(acc_ref)

    def scores(j, masked):
        z = lax.dot_general(q, k2_ref[j], _NT, preferred_element_type=F32) + bias
        sp = _softplus(z)
        if masked:
            sp = jnp.where(causal, sp, 0.0)
        spb = sp.astype(BF16)
        c_loc = [jnp.dot(spb[:, h * t:(h + 1) * t], suffix_ones, preferred_element_type=F32) for h in range(2)]
        return z, c_loc

    def accumulate(j, z, c_loc, masked):
        car = carry_ref[...]
        reps = t // lw
        car_full = jnp.concatenate([car[:, :lw]] * reps + [car[:, lw:]] * reps, axis=1)
        w = jnp.exp(z - jnp.concatenate(c_loc, axis=1) - car_full)
        if masked:
            w = jnp.where(causal, w, 0.0)
        acc_ref[...] += jnp.dot(w.astype(BF16), v2_ref[j], preferred_element_type=F32)
        carry_ref[...] = car + jnp.concatenate([jnp.broadcast_to(c[:, 0:1], (t, lw)) for c in c_loc], axis=1)

    z, c_loc = scores(i, True)
    accumulate(i, z, c_loc, True)

    def pair(s, c):
        j = i - 1 - 2 * s
        za, ca = scores(j, False)
        zb, cb = scores(j - 1, False)
        accumulate(j, za, ca, False)
        accumulate(j - 1, zb, cb, False)
        return c

    lax.fori_loop(0, i // 2, pair, 0)

    @pl.when(i % 2 == 1)
    def _():
        z, c_loc = scores(0, False)
        accumulate(0, z, c_loc, False)

    o_ref[...] = acc_ref[...].astype(o_ref.dtype)


def _sb_attention_prompt(qb, kb, vb, bias, n_seq, seq_len):
    t = 256
    nq = seq_len // t
    lw = 2 * SB_HEAD_DIM
    kern = functools.partial(_sb_attn_kernel, t, nq)
    scratch = 2 * _nbytes((seq_len * 2, lw), BF16) + _nbytes((t, 2 * lw), F32) + _nbytes((t, lw), F32)
    return pl.pallas_call(
        kern, grid=(n_seq, SB_HEADS // 2, nq),
        in_specs=[pl.BlockSpec(memory_space=pltpu.SMEM),
                  pl.BlockSpec((t, lw), lambda b, h, i: (b * nq + i, h)),
                  pl.BlockSpec((seq_len, lw), lambda b, h, i: (b, h)),
                  pl.BlockSpec((seq_len, lw), lambda b, h, i: (b, h))],
        out_specs=pl.BlockSpec((t, lw), lambda b, h, i: (b * nq + i, h)),
        out_shape=jax.ShapeDtypeStruct(qb.shape, BF16),
        scratch_shapes=[pltpu.VMEM((nq, 2 * t, lw), BF16), pltpu.VMEM((nq, 2 * t, lw), BF16),
                        pltpu.VMEM((t, 2 * lw), F32), pltpu.VMEM((t, lw), F32)],
        compiler_params=_params(3, scratch + 4 * _nbytes((seq_len, lw), BF16) + (16 << 20)),
        name="sb_attn_prompt",
    )(bias, qb, kb, vb)


SB_PAGES_PER_STEP = 4


def _sb_sample_kernel(n_steps, pt_ref, q_ref, bias_ref, *refs):
    ppb = SB_PAGES_PER_STEP
    k_refs, v_refs = refs[:ppb], refs[ppb:2 * ppb]
    o_ref, acc_ref, carry_ref = refs[2 * ppb:]
    p = pl.program_id(1)
    nh = SB_HEADS

    @pl.when(p == 0)
    def _():
        acc_ref[...] = jnp.zeros_like(acc_ref)
        carry_ref[...] = jnp.zeros_like(carry_ref)

    qc = q_ref[0] * (SB_HEAD_DIM ** -0.5)
    ri = lax.broadcasted_iota(jnp.int32, (2 * PAGE_SIZE, PAGE_SIZE), 0) & (PAGE_SIZE - 1)
    ci = lax.broadcasted_iota(jnp.int32, (2 * PAGE_SIZE, PAGE_SIZE), 1)
    suffix_ones2 = jnp.where(ri >= ci, 1.0, 0.0).astype(BF16)
    bias = bias_ref[...]

    for k_ref, v_ref in zip(k_refs, v_refs):
        z = jnp.sum(k_ref[0] * qc, axis=1) + bias
        hi, lo = _split_hi_lo(_softplus(z))
        c_loc = jnp.dot(jnp.concatenate([hi, lo], axis=1), suffix_ones2, preferred_element_type=F32)
        car = carry_ref[...]
        w = jnp.exp(z - c_loc - car)
        acc_ref[...] += w[:, None, :] * v_ref[0]
        carry_ref[...] = car + jnp.broadcast_to(c_loc[:, 0:1], (nh, PAGE_SIZE))

    @pl.when(p == n_steps - 1)
    def _():
        o_ref[0] = jnp.sum(acc_ref[...], axis=-1, keepdims=True)


def _sb_attention_sample(q, cache_k, cache_v, page_table, bias):
    bsz, n_pages = page_table.shape
    ppb = SB_PAGES_PER_STEP
    n_steps = n_pages // ppb
    nh, hd = SB_HEADS, SB_HEAD_DIM
    kt = jnp.transpose(cache_k, (0, 2, 3, 1))
    vt = jnp.transpose(cache_v, (0, 2, 3, 1))

    def page_spec(off):
        return pl.BlockSpec((1, nh, hd, PAGE_SIZE), lambda b, p, pt: (pt[b, n_pages - 1 - off - ppb * p], 0, 0, 0))

    pages = [page_spec(off) for off in range(ppb)]
    grid_spec = pltpu.PrefetchScalarGridSpec(
        num_scalar_prefetch=1, grid=(bsz, n_steps),
        in_specs=[pl.BlockSpec((1, nh, hd, 1), lambda b, p, pt: (b, 0, 0, 0)),
                  pl.BlockSpec((nh, 1), lambda b, p, pt: (0, 0))] + pages + pages,
        out_specs=pl.BlockSpec((1, nh, hd, 1), lambda b, p, pt: (b, 0, 0, 0)),
        scratch_shapes=[pltpu.VMEM((nh, hd, PAGE_SIZE), F32), pltpu.VMEM((nh, PAGE_SIZE), F32)])
    out = pl.pallas_call(
        functools.partial(_sb_sample_kernel, n_steps), grid_spec=grid_spec,
        out_shape=jax.ShapeDtypeStruct((bsz, nh, hd, 1), F32),
        compiler_params=_params(2, (4 * ppb + 4) * _nbytes((nh, hd, PAGE_SIZE), F32)),
        name="sb_attn_sample",
    )(page_table, q.reshape(bsz, nh, hd, 1), bias.reshape(nh, 1), *([kt] * ppb), *([vt] * ppb))
    return out.reshape(bsz, nh * hd)


def _gla_in_kernel(x_ref, w_ref, wg_ref, bg_ref, q_ref, k_ref, v_ref, r_ref, la_ref):
    cw = MXU_COLS
    xb = x_ref[...].astype(BF16)

    def cols(lo, n):
        return jnp.dot(xb, w_ref[:, lo:lo + n], preferred_element_type=F32)

    for c in range(GLA_QK // cw):
        q_ref[:, c * cw:(c + 1) * cw] = cols(c * cw, cw) * (GLA_DK ** -0.5)
        k_ref[:, c * cw:(c + 1) * cw] = cols(GLA_QK + c * cw, cw)
    for c in range(GLA_V // cw):
        v_ref[:, c * cw:(c + 1) * cw] = cols(2 * GLA_QK + c * cw, cw)
        r_ref[:, c * cw:(c + 1) * cw] = cols(2 * GLA_QK + GLA_V + c * cw, cw)
    g_low = cols(2 * GLA_QK + 2 * GLA_V, 128).astype(BF16)
    u = jnp.dot(g_low, wg_ref[...], preferred_element_type=F32) + bg_ref[...]
    la_ref[...] = -_softplus(-u) * (1.0 / GLA_TAU)


def _gla_in(x, w_pad, wg_pad, bg, *, tm=512):
    m = x.shape[0]
    tm = min(tm, m)
    n = w_pad.shape[1]
    widths = (GLA_QK, GLA_QK, GLA_V, GLA_V, GLA_QK)
    return pl.pallas_call(
        _gla_in_kernel, grid=(m // tm,),
        in_specs=[pl.BlockSpec((tm, D_MODEL), lambda i: (i, 0)), _resident((D_MODEL, n)),
                  _resident((128, GLA_QK)), _resident((1, GLA_QK))],
        out_specs=[pl.BlockSpec((tm, wd), lambda i: (i, 0)) for wd in widths],
        out_shape=[jax.ShapeDtypeStruct((m, wd), F32) for wd in widths],
        compiler_params=_params(1, _nbytes((D_MODEL, n), BF16) + 12 * _nbytes((tm, D_MODEL), F32)),
        name="gla_in",
    )(x, w_pad, wg_pad, bg.reshape(1, GLA_QK))


def _gla_kernel(t_rows, n_t, q_ref, k_ref, la_ref, v_ref, r_ref, gn_ref, o_ref, s_ref, st_ref):
    t = pl.program_id(2)
    ch = GLA_CHUNK

    @pl.when(t == 0)
    def _():
        st_ref[...] = jnp.zeros_like(st_ref)

    ri = lax.broadcasted_iota(jnp.int32, (ch, ch), 0)
    ci = lax.broadcasted_iota(jnp.int32, (ch, ch), 1)
    tri = ci <= ri
    prefix_ones = jnp.where(tri, 1.0, 0.0).astype(BF16)
    prefix_ones2 = jnp.concatenate([prefix_ones, prefix_ones], axis=1)
    gn = gn_ref[...]
    for c in range(t_rows // ch):
        sl = slice(c * ch, (c + 1) * ch)
        hi, lo = _split_hi_lo(la_ref[sl, :])
        b = jnp.dot(prefix_ones2, jnp.concatenate([hi, lo], axis=0), preferred_element_type=F32)
        mid = b[ch // 2 - 1:ch // 2, :]
        last = b[ch - 1:ch, :]
        qs = q_ref[sl, :] * jnp.exp(b - mid)
        ks = k_ref[sl, :] * jnp.exp(mid - b)
        scores = lax.dot_general(qs.astype(BF16), ks.astype(BF16), _NT, preferred_element_type=F32)
        scores = jnp.where(tri, scores, 0.0)
        vb = v_ref[sl, :].astype(BF16)
        st = st_ref[...]
        qe = (qs * jnp.exp(mid)).astype(BF16)
        o = (jnp.dot(scores.astype(BF16), vb, preferred_element_type=F32)
             + lax.dot_general(qe, st.astype(BF16), _NT, preferred_element_type=F32))
        kd = (ks * jnp.exp(last - mid)).astype(BF16)
        st_ref[...] = jnp.exp(last) * st + lax.dot_general(vb, kd, _TN, preferred_element_type=F32)
        o = o * lax.rsqrt(jnp.mean(o * o, axis=-1, keepdims=True) + LN_EPS) * gn
        r = r_ref[sl, :]
        o_ref[sl, :] = (o * (r * _sigmoid(r))).astype(o_ref.dtype)

    @pl.when(t == n_t - 1)
    def _():
        s_ref[0, 0] = st_ref[...].T


def _gla_prompt(q, k, la, v, r, g_norm, n_seq, seq_len, *, t_rows=512):
    n_t = seq_len // t_rows
    kern = functools.partial(_gla_kernel, t_rows, n_t)
    qk_spec = pl.BlockSpec((t_rows, GLA_DK), lambda b, h, t: (b * n_t + t, h))
    v_spec = pl.BlockSpec((t_rows, GLA_DV), lambda b, h, t: (b * n_t + t, h))
    return pl.pallas_call(
        kern, grid=(n_seq, GLA_HEADS, n_t),
        in_specs=[qk_spec, qk_spec, qk_spec, v_spec, v_spec, pl.BlockSpec((1, GLA_DV), lambda b, h, t: (0, 0))],
        out_specs=[v_spec, pl.BlockSpec((1, 1, GLA_DK, GLA_DV), lambda b, h, t: (b, h, 0, 0))],
        out_shape=[jax.ShapeDtypeStruct((n_seq * seq_len, GLA_V), BF16),
                   jax.ShapeDtypeStruct((n_seq, GLA_HEADS, GLA_DK, GLA_DV), F32)],
        scratch_shapes=[pltpu.VMEM((GLA_DV, GLA_DK), F32)],
        compiler_params=_params(3, 16 * _nbytes((t_rows, GLA_DV), F32)),
        name="gla_prompt",
    )(q, k, la, v, r, g_norm.reshape(1, GLA_DV))


def _gla_sample_kernel(q_ref, k_ref, la_ref, v_ref, r_ref, gn_ref, s0_ref, o_ref, s_ref):
    gn = gn_ref[...]
    for h in range(GLA_HEADS):
        a = jnp.exp(la_ref[0, h])
        vrow = v_ref[0, :, h * GLA_DV:(h + 1) * GLA_DV]
        s_new = a * s0_ref[0, h] + k_ref[0, h] * vrow
        s_ref[0, h] = s_new
        o = jnp.sum(q_ref[0, h] * s_new, axis=0, keepdims=True)
        o = o * lax.rsqrt(jnp.mean(o * o, axis=-1, keepdims=True) + LN_EPS) * gn
        r = r_ref[0, :, h * GLA_DV:(h + 1) * GLA_DV]
        o_ref[0, :, h * GLA_DV:(h + 1) * GLA_DV] = o * (r * _sigmoid(r))


def _gla_sample(q, k, la, v, r, g_norm, s0):
    bsz = q.shape[0]
    col = lambda t: t.reshape(bsz, GLA_HEADS, GLA_DK, 1)
    col_spec = pl.BlockSpec((1, GLA_HEADS, GLA_DK, 1), lambda b: (b, 0, 0, 0))
    row_spec = pl.BlockSpec((1, 1, GLA_V), lambda b: (b, 0, 0))
    st_spec = pl.BlockSpec((1, GLA_HEADS, GLA_DK, GLA_DV), lambda b: (b, 0, 0, 0))
    o, s = pl.pallas_call(
        _gla_sample_kernel, grid=(bsz,),
        in_specs=[col_spec, col_spec, col_spec, row_spec, row_spec,
                  pl.BlockSpec((1, GLA_DV), lambda b: (0, 0)), st_spec],
        out_specs=[row_spec, st_spec],
        out_shape=[jax.ShapeDtypeStruct((bsz, 1, GLA_V), F32), jax.ShapeDtypeStruct(s0.shape, F32)],
        compiler_params=_params(1, 8 * _nbytes((GLA_HEADS, GLA_DK, GLA_DV), F32)),
        name="gla_sample",
    )(col(q), col(k), col(la), v.reshape(bsz, 1, GLA_V), r.reshape(bsz, 1, GLA_V), g_norm.reshape(1, GLA_DV), s0)
    return o.reshape(bsz, GLA_V), s


def _dsw_attn_kernel(d, bpc, slope_ref, q_ref, kp_ref, kc_ref, vp_ref, vc_ref, o_ref, lse_ref):
    t = pl.program_id(1)
    sb = pl.program_id(2)
    qb = DSW_QB
    hd = DSW_HEAD_DIM
    has_prev = jnp.minimum(t + (sb & (bpc - 1)), 1)
    row = lax.broadcasted_iota(jnp.int32, (qb, 2 * qb), 0)
    col = lax.broadcasted_iota(jnp.int32, (qb, 2 * qb), 1)
    steps_back = row + qb - col
    first_col = qb - has_prev * qb
    valid = (steps_back >= 0) & (steps_back <= qb) & (col >= first_col)
    dist = steps_back.astype(F32) * float(d)
    q = q_ref[...]
    kk = jnp.concatenate([kp_ref[...], kc_ref[...]], axis=0)
    vv = jnp.concatenate([vp_ref[...], vc_ref[...]], axis=0)
    for h in range(DSW_HEADS):
        hs = slice(h * hd, (h + 1) * hd)
        s = lax.dot_general(q[:, hs], kk[:, hs], _NT, preferred_element_type=F32) - slope_ref[h] * dist
        s = jnp.where(valid, s, -jnp.inf)
        m = jnp.max(s, axis=-1, keepdims=True)
        p = jnp.exp(s - m)
        l = jnp.sum(p, axis=-1, keepdims=True)
        o = jnp.dot(p.astype(BF16), vv[:, hs], preferred_element_type=F32) / l
        o_ref[:, hs] = o
        lse_ref[:, hs] = jnp.broadcast_to(m + jnp.log(l), (qb, hd))


def _dsw_attention_prompt(d, qc, kc, vc, slopes, n_seq, seq_len):
    qb = DSW_QB
    n_tiles = seq_len // DSW_TILE
    nsb = DSW_TILE // qb
    bpc = nsb // d

    def cur(b, t, sb):
        return ((b * n_tiles + t) * nsb + sb, 0)

    def prev(b, t, sb):
        first = (sb & (bpc - 1)) == 0
        back = jnp.where(first, jnp.where(t > 0, nsb - bpc + 1, 0), 1)
        return ((b * n_tiles + t) * nsb + sb - back, 0)

    def nat(b, t, sb):
        return ((b * n_tiles + t) * bpc + (sb & (bpc - 1)), sb // bpc)

    blk = lambda im: pl.BlockSpec((qb, DSW_INNER), im)
    m = n_seq * seq_len
    o, lse = pl.pallas_call(
        functools.partial(_dsw_attn_kernel, d, bpc), grid=(n_seq, n_tiles, nsb),
        in_specs=[pl.BlockSpec(memory_space=pltpu.SMEM), blk(cur), blk(prev), blk(cur), blk(prev), blk(cur)],
        out_specs=[blk(nat), blk(nat)],
        out_shape=[jax.ShapeDtypeStruct((m // d, d * DSW_INNER), F32)] * 2,
        compiler_params=_params(3, 32 * _nbytes((qb, DSW_INNER), F32)),
        name=f"dsw_attn_prompt_d{d}",
    )(slopes, qc, kc, kc, vc, vc)
    return o.reshape(m, DSW_INNER), lse.reshape(m, DSW_INNER)


def _dsw_outln_kernel(o0_ref, o1_ref, o2_ref, l0_ref, l1_ref, l2_ref, w_ref, x_ref, g_ref, b_ref, y_ref):
    l0, l1, l2 = l0_ref[...], l1_ref[...], l2_ref[...]
    m = jnp.maximum(jnp.maximum(l0, l1), l2)
    e0, e1, e2 = jnp.exp(l0 - m), jnp.exp(l1 - m), jnp.exp(l2 - m)
    o = (o0_ref[...] * e0 + o1_ref[...] * e1 + o2_ref[...] * e2) / (e0 + e1 + e2)
    mix = jnp.dot(o.astype(BF16), w_ref[...], preferred_element_type=F32)
    y_ref[...] = _layer_norm(DN_ALPHA * x_ref[...] + mix, g_ref[...], b_ref[...])


def _dsw_out_ln(os_, lses, w, x, g, b, *, tm=512):
    m = x.shape[0]
    tm = min(tm, m)
    part = pl.BlockSpec((tm, DSW_INNER), lambda i: (i, 0))
    row = pl.BlockSpec((tm, D_MODEL), lambda i: (i, 0))
    return pl.pallas_call(
        _dsw_outln_kernel, grid=(m // tm,),
        in_specs=[part] * 6 + [_resident((DSW_INNER, D_MODEL)), row, _resident((1, D_MODEL)), _resident((1, D_MODEL))],
        out_specs=row,
        out_shape=jax.ShapeDtypeStruct((m, D_MODEL), F32),
        compiler_params=_params(1, 24 * _nbytes((tm, DSW_INNER), F32) + 8 * _nbytes((tm, D_MODEL), F32)),
        name="dsw_out_ln",
    )(*os_, *lses, w, x, g.reshape(1, D_MODEL), b.reshape(1, D_MODEL))


def _dsw_sample_kernel(w, d, hps, slope_ref, q_ref, new_ref, buf_ref, out_ref, o_ref, lse_ref):
    hb = pl.program_id(1)
    pos = lax.broadcasted_iota(jnp.int32, (1, w), 1)
    on_dilation = (pos & (d - 1)) == 0
    dist = (w - pos).astype(F32)
    is_last = pos == w - 1
    for hh in range(hps):
        kt = buf_ref[0, 0, hh]
        vt = buf_ref[0, 1, hh]
        qc = q_ref[0, hh] * (DSW_HEAD_DIM ** -0.5)
        k_new = new_ref[0, 0, hh]
        v_new = new_ref[0, 1, hh]
        s = jnp.sum(kt * qc, axis=0, keepdims=True) - slope_ref[hb * hps + hh] * dist
        s = jnp.where(on_dilation, s, -jnp.inf)
        s_new = jnp.sum(qc * k_new, axis=0, keepdims=True)
        m = jnp.maximum(jnp.max(s, axis=-1, keepdims=True), s_new)
        p = jnp.exp(s - m)
        p_new = jnp.exp(s_new - m)
        l = jnp.sum(p, axis=-1, keepdims=True) + p_new
        o_ref[0, hh] = (jnp.sum(vt * p, axis=-1, keepdims=True) + p_new * v_new) / l
        lse_ref[0, hh] = m + jnp.log(l)
        out_ref[0, 0, hh] = jnp.where(is_last, k_new, pltpu.roll(kt, w - 1, axis=1))
        out_ref[0, 1, hh] = jnp.where(is_last, v_new, pltpu.roll(vt, w - 1, axis=1))


def _dsw_sample_group(w, d, q, new_kv, buf, slopes):
    bsz = q.shape[0]
    nh, hd = DSW_HEADS, DSW_HEAD_DIM
    hps = min(nh, max(1, 4096 // w))
    buf_t = jnp.transpose(buf, (0, 2, 3, 4, 1))
    buf_spec = pl.BlockSpec((1, 2, hps, hd, w), lambda b, h: (b, 0, h, 0, 0))
    out_t, o, lse = pl.pallas_call(
        functools.partial(_dsw_sample_kernel, w, d, hps), grid=(bsz, nh // hps),
        in_specs=[pl.BlockSpec(memory_space=pltpu.SMEM),
                  pl.BlockSpec((1, hps, hd, 1), lambda b, h: (b, h, 0, 0)),
                  pl.BlockSpec((1, 2, hps, hd, 1), lambda b, h: (b, 0, h, 0, 0)),
                  buf_spec],
        out_specs=[buf_spec, pl.BlockSpec((1, hps, hd, 1), lambda b, h: (b, h, 0, 0)),
                   pl.BlockSpec((1, hps, 1, 1), lambda b, h: (b, h, 0, 0))],
        out_shape=[jax.ShapeDtypeStruct(buf_t.shape, F32), jax.ShapeDtypeStruct((bsz, nh, hd, 1), F32),
                   jax.ShapeDtypeStruct((bsz, nh, 1, 1), F32)],
        compiler_params=_params(2, 6 * _nbytes((2, hps, hd, w), F32) + (4 << 20)),
        name=f"dsw_sample_d{d}",
    )(slopes, q, new_kv, buf_t)
    return o.reshape(bsz, nh * hd), lse.reshape(bsz, nh), jnp.transpose(out_t, (0, 4, 1, 2, 3))


def kernel(x_prompt, x_sample, cache_sc_conv, cache_sb_k, cache_sb_v, state_gla, cache_dsw_kv0, cache_dsw_kv1,
           cache_dsw_kv2, state_ffn_conv, page_table, ln_g, ln_b, w_sc_in, w_sc_conv, w_sc_out, w_sb_qkv, w_sb_out,
           b_sb, w_gla_in, w_gla_gate_up, b_gla_gate, g_gla_norm, w_gla_out, w_dsw_qkv, w_dsw_out, w_ffn_up,
           w_ffn_conv, w_ffn_down):
    n_seq, seq_len, _ = x_prompt.shape
    bsz = x_sample.shape[0]
    xp = x_prompt.reshape(n_seq * seq_len, D_MODEL)
    xs = x_sample.reshape(bsz, D_MODEL)
    bf = lambda w: w.astype(BF16)
    ffn_p, ffn_s = [], []

    def ffn(layer, xp, xs):
        w_up, w_down = bf(w_ffn_up[layer]), bf(w_ffn_down[layer])
        g, b = ln_g[layer, 1], ln_b[layer, 1]
        xp, st_p = _gconv("ffn", xp, w_up, w_ffn_conv[layer], w_down, g, b, seq_len=seq_len)
        prev = state_ffn_conv[layer]
        xs, u_s = _gconv("ffn", xs, w_up, w_ffn_conv[layer], w_down, g, b, prev=(prev[:, 0], prev[:, 1]))
        ffn_p.append(st_p[:, 6:8])
        ffn_s.append(jnp.stack([prev[:, 1], u_s], axis=1))
        return xp, xs

    w_in, w_out = bf(w_sc_in), bf(w_sc_out)
    xp, sc_st = _gconv("sc", xp, w_in, w_sc_conv, w_out, ln_g[0, 0], ln_b[0, 0], seq_len=seq_len)
    xs, sc_u = _gconv("sc", xs, w_in, w_sc_conv, w_out, ln_g[0, 0], ln_b[0, 0],
                      prev=(cache_sc_conv[:, 0], cache_sc_conv[:, 1]))
    sc_p = sc_st[:, 6:8]
    sc_s = jnp.stack([cache_sc_conv[:, 1], sc_u], axis=1)
    xp, xs = ffn(0, xp, xs)

    w_qkv, w_out = bf(w_sb_qkv), bf(w_sb_out)
    width = SB_HEADS * SB_HEAD_DIM
    scale = SB_HEAD_DIM ** -0.5
    qb, kf, vf, kb, vb = _proj(xp, w_qkv, [(0, width, scale, BF16), (width, width, 1.0, F32),
                                           (2 * width, width, 1.0, F32), (width, width, 1.0, BF16),
                                           (2 * width, width, 1.0, BF16)], name="sb_qkv_prompt")
    o = _sb_attention_prompt(qb, kb, vb, b_sb, n_seq, seq_len)
    sb_k_p = kf.reshape(n_seq, seq_len, SB_HEADS, SB_HEAD_DIM)
    sb_v_p = vf.reshape(n_seq, seq_len, SB_HEADS, SB_HEAD_DIM)
    xp = _out_ln(o, w_out, xp, ln_g[1, 0], ln_b[1, 0])
    qs, ks, vs = _proj(xs, w_qkv, [(0, width, 1.0, F32), (width, width, 1.0, F32), (2 * width, width, 1.0, F32)],
                       name="sb_qkv_sample")
    o = _sb_attention_sample(qs, cache_sb_k, cache_sb_v, page_table, b_sb)
    sb_k_s = ks.reshape(bsz, 1, SB_HEADS, SB_HEAD_DIM)
    sb_v_s = vs.reshape(bsz, 1, SB_HEADS, SB_HEAD_DIM)
    xs = _out_ln(o, w_out, xs, ln_g[1, 0], ln_b[1, 0])
    xp, xs = ffn(1, xp, xs)

    n_in = w_gla_in.shape[1]
    w_in = bf(jnp.pad(w_gla_in, ((0, 0), (0, 2 * GLA_QK + 2 * GLA_V + 128 - n_in))))
    w_gate = bf(jnp.pad(w_gla_gate_up, ((0, 128 - GLA_RANK), (0, 0))))
    w_out = bf(w_gla_out)
    q, k, v, r, la = _gla_in(xp, w_in, w_gate, b_gla_gate)
    o, gla_p = _gla_prompt(q, k, la, v, r, g_gla_norm, n_seq, seq_len)
    xp = _out_ln(o, w_out, xp, ln_g[2, 0], ln_b[2, 0])
    q, k, v, r, la = _gla_in(xs, w_in, w_gate, b_gla_gate)
    o, gla_s = _gla_sample(q, k, la, v, r, g_gla_norm, state_gla)
    xs = _out_ln(o, w_out, xs, ln_g[2, 0], ln_b[2, 0])
    xp, xs = ffn(2, xp, xs)

    w_qkv, w_out = bf(w_dsw_qkv), bf(w_dsw_out)
    ng = len(DSW_GROUPS)
    n_heads = ng * DSW_HEADS
    slopes = (2.0 ** (-8.0 * jnp.arange(1, n_heads + 1, dtype=F32) / n_heads)).reshape(ng, DSW_HEADS)
    m = n_seq * seq_len
    tm = 512
    os_, lses = [], []
    for g, (_, d) in enumerate(DSW_GROUPS):
        rows_per_class = DSW_TILE // d
        parts = max(1, tm // rows_per_class)
        steps_per_tile = DSW_TILE // tm
        x_map = (lambda i: (i, 0)) if d == 1 else (lambda i: (i // steps_per_tile, i % steps_per_tile))
        base = g * 3 * DSW_INNER
        qc, kc, vc = _proj(xp.reshape(m // d, d * D_MODEL), w_qkv,
                           [(base, DSW_INNER, DSW_HEAD_DIM ** -0.5, BF16), (base + DSW_INNER, DSW_INNER, 1.0, BF16),
                            (base + 2 * DSW_INNER, DSW_INNER, 1.0, BF16)],
                           tm=tm, name=f"dsw_qkv_prompt_d{d}", row_blocks=m // tm, x_map=x_map, parts=parts)
        o, lse = _dsw_attention_prompt(d, qc, kc, vc, slopes[g], n_seq, seq_len)
        os_.append(o)
        lses.append(lse)
    w_max = max(w for w, _ in DSW_GROUPS)
    per_seq, tail = seq_len // tm, w_max // tm
    kv_tail = _proj(xp, w_qkv, [(g * 3 * DSW_INNER + DSW_INNER, 2 * DSW_INNER, 1.0, F32) for g in range(ng)],
                    tm=tm, name="dsw_kv_tail", row_blocks=n_seq * tail,
                    x_map=lambda i: ((i // tail) * per_seq + per_seq - tail + i % tail, 0))
    dsw_p = [kv.reshape(n_seq, w_max, 2, DSW_HEADS, DSW_HEAD_DIM)[:, w_max - w:]
             for kv, (w, _) in zip(kv_tail, DSW_GROUPS)]
    xp = _dsw_out_ln(os_, lses, w_out, xp, ln_g[3, 0], ln_b[3, 0])
    bufs = [cache_dsw_kv0, cache_dsw_kv1, cache_dsw_kv2]
    (qkv,) = _proj(xs, w_qkv, [(0, w_qkv.shape[1], 1.0, F32)], name="dsw_qkv_sample")
    qkv = qkv.reshape(bsz, ng, 3, DSW_HEADS, DSW_HEAD_DIM, 1)
    os_, lses, dsw_s = [], [], []
    for g, (w, d) in enumerate(DSW_GROUPS):
        o, lse, rolled = _dsw_sample_group(w, d, qkv[:, g, 0], qkv[:, g, 1:3], bufs[g], slopes[g])
        os_.append(o)
        lses.append(jnp.repeat(lse, DSW_HEAD_DIM, axis=1))
        dsw_s.append(rolled)
    xs = _dsw_out_ln(os_, lses, w_out, xs, ln_g[3, 0], ln_b[3, 0])
    xp, xs = ffn(3, xp, xs)

    return (xp.reshape(n_seq, seq_len, D_MODEL), xs.reshape(bsz, 1, D_MODEL), sc_p, sc_s, sb_k_p, sb_v_p, sb_k_s,
            sb_v_s, gla_p, gla_s, dsw_p[0], dsw_p[1], dsw_p[2], dsw_s[0], dsw_s[1], dsw_s[2],
            jnp.stack(ffn_p), jnp.stack(ffn_s))
```

```python
import functools

import jax
import jax.numpy as jnp
from jax import lax
from jax.experimental import pallas as pl
from jax.experimental.pallas import tpu as pltpu

F32 = jnp.float32
BF16 = jnp.bfloat16

D_MODEL = 1024
DEPTH = 4
DN_ALPHA = (2.0 * DEPTH) ** 0.25
LN_EPS = 1e-5
PAGE_SIZE = 128

SB_HEADS = 16
SB_HEAD_DIM = 64

GLA_HEADS = 4
GLA_DK = 128
GLA_DV = 256
GLA_RANK = 16
GLA_TAU = 16.0
GLA_QK = GLA_HEADS * GLA_DK
GLA_V = GLA_HEADS * GLA_DV
GLA_CHUNK = 128

DSW_GROUPS = ((128, 1), (512, 4), (2048, 16))
DSW_HEADS = 8
DSW_HEAD_DIM = 64
DSW_INNER = DSW_HEADS * DSW_HEAD_DIM
DSW_TILE = 2048
DSW_QB = 128

D_FF = 2816
MXU_COLS = 256
V7X_VMEM_BYTES = 64 * 1024 * 1024
VMEM_CAP = V7X_VMEM_BYTES - 6 * 1024 * 1024


def _vmem_limit(block_bytes):
    return int(min(VMEM_CAP, block_bytes * 5 // 4 + (8 << 20)))


def _params(n_grid, block_bytes):
    return pltpu.CompilerParams(dimension_semantics=("arbitrary",) * n_grid,
                                vmem_limit_bytes=_vmem_limit(block_bytes))


def _resident(shape):
    nd = len(shape)
    return pl.BlockSpec(shape, lambda *_: (0,) * nd, pipeline_mode=pl.Buffered(1))


def _nbytes(shape, dtype):
    n = 1
    for s in shape:
        n *= s
    return n * jnp.dtype(dtype).itemsize


def _layer_norm(y, g, b):
    mu = jnp.mean(y, axis=-1, keepdims=True)
    yc = y - mu
    var = jnp.mean(yc * yc, axis=-1, keepdims=True)
    return yc * lax.rsqrt(var + LN_EPS) * g + b


def _sigmoid(z):
    return 1.0 / (1.0 + jnp.exp(-z))


def _softplus(z):
    return jnp.maximum(z, 0.0) + jnp.log(1.0 + jnp.exp(-jnp.abs(z)))


def _split_hi_lo(x):
    hi = x.astype(BF16)
    lo = (x - hi.astype(F32)).astype(BF16)
    return hi, lo


_NT = (((1,), (1,)), ((), ()))
_TN = (((0,), (0,)), ((), ()))


def _gconv_kernel(kind, step_mode, tm, fdim, tiles_per_seq, *refs):
    cw = MXU_COLS
    if step_mode:
        x_ref, w1_ref, wc_ref, w2_ref, g_ref, b_ref, p0_ref, p1_ref, y_ref, st_ref, act_ref = refs
    else:
        x_ref, w1_ref, wc_ref, w2_ref, g_ref, b_ref, y_ref, st_ref, act_ref, carry_ref, ubuf_ref = refs

        @pl.when(pl.program_id(0) % tiles_per_seq == 0)
        def _():
            carry_ref[...] = jnp.zeros_like(carry_ref)

    x = x_ref[...]
    xb = x.astype(BF16)
    for c in range(fdim // cw):
        lo = c * cw

        def proj(k, lo=lo):
            return jnp.dot(xb, w1_ref[:, k * fdim + lo:k * fdim + lo + cw], preferred_element_type=F32)

        if kind == "ffn":
            u = proj(0)
            other = proj(1)
        else:
            other = proj(0)
            u = proj(1) * proj(2)
        w0 = wc_ref[0:1, lo:lo + cw]
        w1 = wc_ref[1:2, lo:lo + cw]
        w2 = wc_ref[2:3, lo:lo + cw]
        if step_mode:
            z = w0 * p0_ref[:, lo:lo + cw] + w1 * p1_ref[:, lo:lo + cw] + w2 * u
            st_ref[:, lo:lo + cw] = u
        else:
            ubuf_ref[0:8, :] = carry_ref[:, lo:lo + cw]
            ubuf_ref[8:tm + 8, :] = u
            z = w0 * ubuf_ref[6:tm + 6, :] + w1 * ubuf_ref[7:tm + 7, :] + w2 * u
            tail = u[tm - 8:tm, :]
            carry_ref[:, lo:lo + cw] = tail
            st_ref[0, :, lo:lo + cw] = tail
        if kind == "ffn":
            act = z * _sigmoid(z) * other
        else:
            act = other * z
        act_ref[:, lo:lo + cw] = act.astype(BF16)
    mix = jnp.dot(act_ref[...], w2_ref[...], preferred_element_type=F32)
    y_ref[...] = _layer_norm(DN_ALPHA * x + mix, g_ref[...], b_ref[...])


def _gconv(kind, x, w1, wc, w2, g, b, *, seq_len=None, prev=None, tm=512):
    m = x.shape[0]
    fdim = wc.shape[1]
    nw = w1.shape[1]
    step_mode = prev is not None
    g2, b2 = g.reshape(1, D_MODEL), b.reshape(1, D_MODEL)
    common = [_resident((D_MODEL, nw)), _resident((3, fdim)), _resident((fdim, D_MODEL)),
              _resident((1, D_MODEL)), _resident((1, D_MODEL))]
    wbytes = _nbytes((D_MODEL, nw), BF16) + _nbytes((fdim, D_MODEL), BF16)
    if step_mode:
        tm = m
        kern = functools.partial(_gconv_kernel, kind, True, tm, fdim, 1)
        return pl.pallas_call(
            kern, grid=(1,),
            in_specs=[pl.BlockSpec((tm, D_MODEL), lambda i: (0, 0))] + common
            + [pl.BlockSpec((tm, fdim), lambda i: (0, 0))] * 2,
            out_specs=[pl.BlockSpec((tm, D_MODEL), lambda i: (0, 0)), pl.BlockSpec((tm, fdim), lambda i: (0, 0))],
            out_shape=[jax.ShapeDtypeStruct((m, D_MODEL), F32), jax.ShapeDtypeStruct((m, fdim), F32)],
            scratch_shapes=[pltpu.VMEM((tm, fdim), BF16)],
            compiler_params=_params(1, wbytes + 8 * _nbytes((tm, fdim), F32)),
            name=f"gconv_{kind}_step",
        )(x, w1, wc, w2, g2, b2, prev[0], prev[1])
    tiles_per_seq = seq_len // tm
    n_seq = m // seq_len
    kern = functools.partial(_gconv_kernel, kind, False, tm, fdim, tiles_per_seq)
    return pl.pallas_call(
        kern, grid=(m // tm,),
        in_specs=[pl.BlockSpec((tm, D_MODEL), lambda i: (i, 0))] + common,
        out_specs=[pl.BlockSpec((tm, D_MODEL), lambda i: (i, 0)),
                   pl.BlockSpec((1, 8, fdim), lambda i: (i // tiles_per_seq, 0, 0))],
        out_shape=[jax.ShapeDtypeStruct((m, D_MODEL), F32), jax.ShapeDtypeStruct((n_seq, 8, fdim), F32)],
        scratch_shapes=[pltpu.VMEM((tm, fdim), BF16), pltpu.VMEM((8, fdim), F32), pltpu.VMEM((tm + 8, MXU_COLS), F32)],
        compiler_params=_params(1, wbytes + 6 * _nbytes((tm, D_MODEL), F32) + 12 * _nbytes((tm, MXU_COLS), F32)
                                + _nbytes((tm, fdim), BF16)),
        name=f"gconv_{kind}_seq",
    )(x, w1, wc, w2, g2, b2)


def _proj_kernel(n_cols, outs, parts, x_ref, w_ref, *out_refs):
    cw = MXU_COLS
    if parts == 1:
        xb = x_ref[...].astype(BF16)
    else:
        xb = jnp.concatenate([x_ref[:, j * D_MODEL:(j + 1) * D_MODEL].astype(BF16) for j in range(parts)], axis=0)
    for c in range(n_cols // cw):
        lo = c * cw
        val = None
        for (start, width, scale), o_ref in zip(outs, out_refs):
            if start <= lo < start + width:
                if val is None:
                    val = jnp.dot(xb, w_ref[:, lo:lo + cw], preferred_element_type=F32)
                v = val if scale == 1.0 else val * scale
                o_ref[:, lo - start:lo - start + cw] = v.astype(o_ref.dtype)


def _proj(x, w, outs, *, tm=512, name="proj", row_blocks=None, x_map=None, parts=1):
    if x_map is None:
        tm = min(tm, x.shape[0])
        row_blocks, x_map = x.shape[0] // tm, (lambda i: (i, 0))
    m = row_blocks * tm
    n = w.shape[1]
    spec = tuple((s, wd, sc) for s, wd, sc, _ in outs)
    kern = functools.partial(_proj_kernel, n, spec, parts)
    obytes = sum(_nbytes((tm, wd), dt) for _, wd, _, dt in outs)
    return pl.pallas_call(
        kern, grid=(row_blocks,),
        in_specs=[pl.BlockSpec((tm // parts, parts * D_MODEL), x_map), _resident((D_MODEL, n))],
        out_specs=[pl.BlockSpec((tm, wd), lambda i: (i, 0)) for _, wd, _, _ in outs],
        out_shape=[jax.ShapeDtypeStruct((m, wd), dt) for _, wd, _, dt in outs],
        compiler_params=_params(1, _nbytes((D_MODEL, n), BF16) + 2 * obytes + 4 * _nbytes((tm, D_MODEL), F32)),
        name=name,
    )(x, w)


def _proj_t_kernel(n_out, x_ref, wt_ref, *out_refs):
    cw = MXU_COLS
    xb = x_ref[...].astype(BF16)
    width = wt_ref.shape[0] // n_out
    for o, o_ref in enumerate(out_refs):
        for c in range(width // cw):
            rows = wt_ref[o * width + c * cw:o * width + (c + 1) * cw, :]
            o_ref[0, c * cw:(c + 1) * cw, :] = lax.dot_general(rows, xb, _NT, preferred_element_type=F32)


def _proj_t(x, wt, n_out, n_seq, seq_len, *, tail_len=None, tm=512, name="proj_t"):
    tail_len = seq_len if tail_len is None else tail_len
    width = wt.shape[0] // n_out
    per_seq, per_tail = seq_len // tm, tail_len // tm
    spec = pl.BlockSpec((1, width, tm), lambda i: (i // per_tail, 0, i % per_tail))
    return pl.pallas_call(
        functools.partial(_proj_t_kernel, n_out), grid=(n_seq * per_tail,),
        in_specs=[pl.BlockSpec((tm, D_MODEL),
                               lambda i: ((i // per_tail) * per_seq + per_seq - per_tail + i % per_tail, 0)),
                  _resident(wt.shape)],
        out_specs=[spec] * n_out,
        out_shape=[jax.ShapeDtypeStruct((n_seq, width, tail_len), F32)] * n_out,
        compiler_params=_params(1, _nbytes(wt.shape, BF16) + 2 * n_out * _nbytes((width, tm), F32)
                                + 4 * _nbytes((tm, D_MODEL), F32)),
        name=name,
    )(x, wt)


def _outln_kernel(a_ref, w_ref, x_ref, g_ref, b_ref, y_ref):
    mix = jnp.dot(a_ref[...].astype(BF16), w_ref[...], preferred_element_type=F32)
    y_ref[...] = _layer_norm(DN_ALPHA * x_ref[...] + mix, g_ref[...], b_ref[...])


def _out_ln(a, w, x, g, b, *, tm=512):
    m, kin = a.shape
    tm = min(tm, m)
    return pl.pallas_call(
        _outln_kernel, grid=(m // tm,),
        in_specs=[pl.BlockSpec((tm, kin), lambda i: (i, 0)), _resident((kin, D_MODEL)),
                  pl.BlockSpec((tm, D_MODEL), lambda i: (i, 0)), _resident((1, D_MODEL)), _resident((1, D_MODEL))],
        out_specs=pl.BlockSpec((tm, D_MODEL), lambda i: (i, 0)),
        out_shape=jax.ShapeDtypeStruct((m, D_MODEL), F32),
        compiler_params=_params(1, _nbytes((kin, D_MODEL), BF16) + 8 * _nbytes((tm, D_MODEL), F32)),
        name="out_ln",
    )(a, w, x, g.reshape(1, D_MODEL), b.reshape(1, D_MODEL))


def _sb_attn_kernel(t, nkb, bias_ref, q_ref, k_ref, v_ref, o_ref, k2_ref, v2_ref, carry_ref, acc_ref):
    hp = pl.program_id(1)
    i = pl.program_id(2)
    hd = SB_HEAD_DIM
    lw = 2 * hd

    @pl.when(i == 0)
    def _():
        lane = lax.broadcasted_iota(jnp.int32, (t, lw), 1)

        def build(j, c):
            r0 = pl.multiple_of(j * t, t)
            for src, dst in ((k_ref, k2_ref), (v_ref, v2_ref)):
                blk = src[pl.ds(r0, t), :]
                zero = jnp.zeros_like(blk)
                dst[j, 0:t, :] = jnp.where(lane < hd, blk, zero)
                dst[j, t:2 * t, :] = jnp.where(lane >= hd, blk, zero)
            return c

        lax.fori_loop(0, nkb, build, 0)

    lane = lax.broadcasted_iota(jnp.int32, (t, 2 * t), 1)
    row = lax.broadcasted_iota(jnp.int32, (t, 2 * t), 0)
    bias = jnp.where(lane < t, bias_ref[2 * hp], bias_ref[2 * hp + 1])
    causal = (lane & (t - 1)) < row
    ri = lax.broadcasted_iota(jnp.int32, (t, t), 0)
    ci = lax.broadcasted_iota(jnp.int32, (t, t), 1)
    suffix_ones = jnp.where(ri >= ci, 1.0, 0.0).astype(BF16)
    q = q_ref[...]
    carry_ref[...] = jnp.zeros_like(carry_ref)
    acc_ref[...] = jnp.zeros_like(acc_ref)

    def scores(j, masked):
        z = lax.dot_general(q, k2_ref[j], _NT, preferred_element_type=F32) + bias
        sp = _softplus(z)
        if masked:
            sp = jnp.where(causal, sp, 0.0)
        spb = sp.astype(BF16)
        c_loc = [jnp.dot(spb[:, h * t:(h + 1) * t], suffix_ones, preferred_element_type=F32) for h in range(2)]
        return z, c_loc

    def accumulate(j, z, c_loc, masked):
        car = carry_ref[...]
        reps = t // lw
        car_full = jnp.concatenate([car[:, :lw]] * reps + [car[:, lw:]] * reps, axis=1)
        w = jnp.exp(z - jnp.concatenate(c_loc, axis=1) - car_full)
        if masked:
            w = jnp.where(causal, w, 0.0)
        acc_ref[...] += jnp.dot(w.astype(BF16), v2_ref[j], preferred_element_type=F32)
        carry_ref[...] = car + jnp.concatenate([jnp.broadcast_to(c[:, 0:1], (t, lw)) for c in c_loc], axis=1)

    z, c_loc = scores(i, True)
    accumulate(i, z, c_loc, True)

    def pair(s, c):
        j = i - 1 - 2 * s
        za, ca = scores(j, False)
        zb, cb = scores(j - 1, False)
        accumulate(j, za, ca, False)
        accumulate(j - 1, zb, cb, False)
        return c

    lax.fori_loop(0, i // 2, pair, 0)

    @pl.when(i % 2 == 1)
    def _():
        z, c_loc = scores(0, False)
        accumulate(0, z, c_loc, False)

    o_ref[...] = acc_ref[...].astype(o_ref.dtype)


def _sb_attention_prompt(qb, kb, vb, bias, n_seq, seq_len):
    t = 256
    nq = seq_len // t
    lw = 2 * SB_HEAD_DIM
    kern = functools.partial(_sb_attn_kernel, t, nq)
    scratch = 2 * _nbytes((seq_len * 2, lw), BF16) + _nbytes((t, 2 * lw), F32) + _nbytes((t, lw), F32)
    return pl.pallas_call(
        kern, grid=(n_seq, SB_HEADS // 2, nq),
        in_specs=[pl.BlockSpec(memory_space=pltpu.SMEM),
                  pl.BlockSpec((t, lw), lambda b, h, i: (b * nq + i, h)),
                  pl.BlockSpec((seq_len, lw), lambda b, h, i: (b, h)),
                  pl.BlockSpec((seq_len, lw), lambda b, h, i: (b, h))],
        out_specs=pl.BlockSpec((t, lw), lambda b, h, i: (b * nq + i, h)),
        out_shape=jax.ShapeDtypeStruct(qb.shape, BF16),
        scratch_shapes=[pltpu.VMEM((nq, 2 * t, lw), BF16), pltpu.VMEM((nq, 2 * t, lw), BF16),
                        pltpu.VMEM((t, 2 * lw), F32), pltpu.VMEM((t, lw), F32)],
        compiler_params=_params(3, scratch + 4 * _nbytes((seq_len, lw), BF16) + (16 << 20)),
        name="sb_attn_prompt",
    )(bias, qb, kb, vb)


SB_PAGES_PER_STEP = 8


def _sb_sample_kernel(n_steps, pt_ref, q_ref, bias_ref, *refs):
    ppb = SB_PAGES_PER_STEP
    k_refs, v_refs = refs[:ppb], refs[ppb:2 * ppb]
    o_ref, acc_ref, carry_ref = refs[2 * ppb:]
    p = pl.program_id(1)
    nh = SB_HEADS

    @pl.when(p == 0)
    def _():
        acc_ref[...] = jnp.zeros_like(acc_ref)
        carry_ref[...] = jnp.zeros_like(carry_ref)

    qc = q_ref[0] * (SB_HEAD_DIM ** -0.5)
    ri = lax.broadcasted_iota(jnp.int32, (2 * PAGE_SIZE, PAGE_SIZE), 0) & (PAGE_SIZE - 1)
    ci = lax.broadcasted_iota(jnp.int32, (2 * PAGE_SIZE, PAGE_SIZE), 1)
    suffix_ones2 = jnp.where(ri >= ci, 1.0, 0.0).astype(BF16)
    bias = bias_ref[...]

    for k_ref, v_ref in zip(k_refs, v_refs):
        z = jnp.sum(k_ref[0] * qc, axis=1) + bias
        hi, lo = _split_hi_lo(_softplus(z))
        c_loc = jnp.dot(jnp.concatenate([hi, lo], axis=1), suffix_ones2, preferred_element_type=F32)
        car = carry_ref[...]
        w = jnp.exp(z - c_loc - car)
        acc_ref[...] += w[:, None, :] * v_ref[0]
        carry_ref[...] = car + jnp.broadcast_to(c_loc[:, 0:1], (nh, PAGE_SIZE))

    @pl.when(p == n_steps - 1)
    def _():
        o_ref[0] = jnp.sum(acc_ref[...], axis=-1, keepdims=True)


def _sb_attention_sample(q, cache_k, cache_v, page_table, bias):
    bsz, n_pages = page_table.shape
    ppb = SB_PAGES_PER_STEP
    n_steps = n_pages // ppb
    nh, hd = SB_HEADS, SB_HEAD_DIM
    kt = jnp.transpose(cache_k, (0, 2, 3, 1))
    vt = jnp.transpose(cache_v, (0, 2, 3, 1))

    def page_spec(off):
        return pl.BlockSpec((1, nh, hd, PAGE_SIZE), lambda b, p, pt: (pt[b, n_pages - 1 - off - ppb * p], 0, 0, 0))

    pages = [page_spec(off) for off in range(ppb)]
    grid_spec = pltpu.PrefetchScalarGridSpec(
        num_scalar_prefetch=1, grid=(bsz, n_steps),
        in_specs=[pl.BlockSpec((1, nh, hd, 1), lambda b, p, pt: (b, 0, 0, 0)),
                  pl.BlockSpec((nh, 1), lambda b, p, pt: (0, 0))] + pages + pages,
        out_specs=pl.BlockSpec((1, nh, hd, 1), lambda b, p, pt: (b, 0, 0, 0)),
        scratch_shapes=[pltpu.VMEM((nh, hd, PAGE_SIZE), F32), pltpu.VMEM((nh, PAGE_SIZE), F32)])
    out = pl.pallas_call(
        functools.partial(_sb_sample_kernel, n_steps), grid_spec=grid_spec,
        out_shape=jax.ShapeDtypeStruct((bsz, nh, hd, 1), F32),
        compiler_params=_params(2, (4 * ppb + 4) * _nbytes((nh, hd, PAGE_SIZE), F32)),
        name="sb_attn_sample",
    )(page_table, q.reshape(bsz, nh, hd, 1), bias.reshape(nh, 1), *([kt] * ppb), *([vt] * ppb))
    return out.reshape(bsz, nh * hd)


def _gla_in_kernel(x_ref, w_ref, wg_ref, bg_ref, q_ref, k_ref, v_ref, r_ref, la_ref):
    cw = MXU_COLS
    xb = x_ref[...].astype(BF16)

    def cols(lo, n):
        return jnp.dot(xb, w_ref[:, lo:lo + n], preferred_element_type=F32)

    for c in range(GLA_QK // cw):
        q_ref[:, c * cw:(c + 1) * cw] = cols(c * cw, cw) * (GLA_DK ** -0.5)
        k_ref[:, c * cw:(c + 1) * cw] = cols(GLA_QK + c * cw, cw)
    for c in range(GLA_V // cw):
        v_ref[:, c * cw:(c + 1) * cw] = cols(2 * GLA_QK + c * cw, cw)
        r_ref[:, c * cw:(c + 1) * cw] = cols(2 * GLA_QK + GLA_V + c * cw, cw)
    g_low = cols(2 * GLA_QK + 2 * GLA_V, 128).astype(BF16)
    u = jnp.dot(g_low, wg_ref[...], preferred_element_type=F32) + bg_ref[...]
    la_ref[...] = -_softplus(-u) * (1.0 / GLA_TAU)


def _gla_in(x, w_pad, wg_pad, bg, *, tm=512):
    m = x.shape[0]
    tm = min(tm, m)
    n = w_pad.shape[1]
    widths = (GLA_QK, GLA_QK, GLA_V, GLA_V, GLA_QK)
    return pl.pallas_call(
        _gla_in_kernel, grid=(m // tm,),
        in_specs=[pl.BlockSpec((tm, D_MODEL), lambda i: (i, 0)), _resident((D_MODEL, n)),
                  _resident((128, GLA_QK)), _resident((1, GLA_QK))],
        out_specs=[pl.BlockSpec((tm, wd), lambda i: (i, 0)) for wd in widths],
        out_shape=[jax.ShapeDtypeStruct((m, wd), F32) for wd in widths],
        compiler_params=_params(1, _nbytes((D_MODEL, n), BF16) + 12 * _nbytes((tm, D_MODEL), F32)),
        name="gla_in",
    )(x, w_pad, wg_pad, bg.reshape(1, GLA_QK))


def _gla_kernel(t_rows, n_t, q_ref, k_ref, la_ref, v_ref, r_ref, gn_ref, o_ref, s_ref, st_ref):
    t = pl.program_id(2)
    ch = GLA_CHUNK

    @pl.when(t == 0)
    def _():
        st_ref[...] = jnp.zeros_like(st_ref)

    ri = lax.broadcasted_iota(jnp.int32, (ch, ch), 0)
    ci = lax.broadcasted_iota(jnp.int32, (ch, ch), 1)
    tri = ci <= ri
    prefix_ones = jnp.where(tri, 1.0, 0.0).astype(BF16)
    prefix_ones2 = jnp.concatenate([prefix_ones, prefix_ones], axis=1)
    gn = gn_ref[...]
    for c in range(t_rows // ch):
        sl = slice(c * ch, (c + 1) * ch)
        hi, lo = _split_hi_lo(la_ref[sl, :])
        b = jnp.dot(prefix_ones2, jnp.concatenate([hi, lo], axis=0), preferred_element_type=F32)
        mid = b[ch // 2 - 1:ch // 2, :]
        last = b[ch - 1:ch, :]
        qs = q_ref[sl, :] * jnp.exp(b - mid)
        ks = k_ref[sl, :] * jnp.exp(mid - b)
        scores = lax.dot_general(qs.astype(BF16), ks.astype(BF16), _NT, preferred_element_type=F32)
        scores = jnp.where(tri, scores, 0.0)
        vb = v_ref[sl, :].astype(BF16)
        st = st_ref[...]
        qe = (qs * jnp.exp(mid)).astype(BF16)
        o = (jnp.dot(scores.astype(BF16), vb, preferred_element_type=F32)
             + lax.dot_general(qe, st.astype(BF16), _NT, preferred_element_type=F32))
        kd = (ks * jnp.exp(last - mid)).astype(BF16)
        st_ref[...] = jnp.exp(last) * st + lax.dot_general(vb, kd, _TN, preferred_element_type=F32)
        o = o * lax.rsqrt(jnp.mean(o * o, axis=-1, keepdims=True) + LN_EPS) * gn
        r = r_ref[sl, :]
        o_ref[sl, :] = (o * (r * _sigmoid(r))).astype(o_ref.dtype)

    @pl.when(t == n_t - 1)
    def _():
        s_ref[0, 0] = st_ref[...].T


def _gla_prompt(q, k, la, v, r, g_norm, n_seq, seq_len, *, t_rows=512):
    n_t = seq_len // t_rows
    kern = functools.partial(_gla_kernel, t_rows, n_t)
    qk_spec = pl.BlockSpec((t_rows, GLA_DK), lambda b, h, t: (b * n_t + t, h))
    v_spec = pl.BlockSpec((t_rows, GLA_DV), lambda b, h, t: (b * n_t + t, h))
    return pl.pallas_call(
        kern, grid=(n_seq, GLA_HEADS, n_t),
        in_specs=[qk_spec, qk_spec, qk_spec, v_spec, v_spec, pl.BlockSpec((1, GLA_DV), lambda b, h, t: (0, 0))],
        out_specs=[v_spec, pl.BlockSpec((1, 1, GLA_DK, GLA_DV), lambda b, h, t: (b, h, 0, 0))],
        out_shape=[jax.ShapeDtypeStruct((n_seq * seq_len, GLA_V), BF16),
                   jax.ShapeDtypeStruct((n_seq, GLA_HEADS, GLA_DK, GLA_DV), F32)],
        scratch_shapes=[pltpu.VMEM((GLA_DV, GLA_DK), F32)],
        compiler_params=_params(3, 16 * _nbytes((t_rows, GLA_DV), F32)),
        name="gla_prompt",
    )(q, k, la, v, r, g_norm.reshape(1, GLA_DV))


def _gla_sample_kernel(q_ref, k_ref, la_ref, v_ref, r_ref, gn_ref, s0_ref, o_ref, s_ref):
    gn = gn_ref[...]
    for h in range(GLA_HEADS):
        a = jnp.exp(la_ref[0, h])
        vrow = v_ref[0, :, h * GLA_DV:(h + 1) * GLA_DV]
        s_new = a * s0_ref[0, h] + k_ref[0, h] * vrow
        s_ref[0, h] = s_new
        o = jnp.sum(q_ref[0, h] * s_new, axis=0, keepdims=True)
        o = o * lax.rsqrt(jnp.mean(o * o, axis=-1, keepdims=True) + LN_EPS) * gn
        r = r_ref[0, :, h * GLA_DV:(h + 1) * GLA_DV]
        o_ref[0, :, h * GLA_DV:(h + 1) * GLA_DV] = o * (r * _sigmoid(r))


def _gla_sample(q, k, la, v, r, g_norm, s0):
    bsz = q.shape[0]
    col = lambda t: t.reshape(bsz, GLA_HEADS, GLA_DK, 1)
    col_spec = pl.BlockSpec((1, GLA_HEADS, GLA_DK, 1), lambda b: (b, 0, 0, 0))
    row_spec = pl.BlockSpec((1, 1, GLA_V), lambda b: (b, 0, 0))
    st_spec = pl.BlockSpec((1, GLA_HEADS, GLA_DK, GLA_DV), lambda b: (b, 0, 0, 0))
    o, s = pl.pallas_call(
        _gla_sample_kernel, grid=(bsz,),
        in_specs=[col_spec, col_spec, col_spec, row_spec, row_spec,
                  pl.BlockSpec((1, GLA_DV), lambda b: (0, 0)), st_spec],
        out_specs=[row_spec, st_spec],
        out_shape=[jax.ShapeDtypeStruct((bsz, 1, GLA_V), F32), jax.ShapeDtypeStruct(s0.shape, F32)],
        compiler_params=_params(1, 8 * _nbytes((GLA_HEADS, GLA_DK, GLA_DV), F32)),
        name="gla_sample",
    )(col(q), col(k), col(la), v.reshape(bsz, 1, GLA_V), r.reshape(bsz, 1, GLA_V), g_norm.reshape(1, GLA_DV), s0)
    return o.reshape(bsz, GLA_V), s


def _dsw_attn_kernel(d, bpc, slope_ref, q_ref, kp_ref, kc_ref, vp_ref, vc_ref, o_ref, lse_ref):
    t = pl.program_id(1)
    sb = pl.program_id(2)
    qb = DSW_QB
    hd = DSW_HEAD_DIM
    has_prev = jnp.minimum(t + (sb & (bpc - 1)), 1)
    row = lax.broadcasted_iota(jnp.int32, (qb, 2 * qb), 0)
    col = lax.broadcasted_iota(jnp.int32, (qb, 2 * qb), 1)
    steps_back = row + qb - col
    first_col = qb - has_prev * qb
    valid = (steps_back >= 0) & (steps_back <= qb) & (col >= first_col)
    dist = steps_back.astype(F32) * float(d)
    q = q_ref[...]
    kk = jnp.concatenate([kp_ref[...], kc_ref[...]], axis=0)
    vv = jnp.concatenate([vp_ref[...], vc_ref[...]], axis=0)
    for h in range(DSW_HEADS):
        hs = slice(h * hd, (h + 1) * hd)
        s = lax.dot_general(q[:, hs], kk[:, hs], _NT, preferred_element_type=F32) - slope_ref[h] * dist
        s = jnp.where(valid, s, -jnp.inf)
        m = jnp.max(s, axis=-1, keepdims=True)
        p = jnp.exp(s - m)
        l = jnp.sum(p, axis=-1, keepdims=True)
        o = jnp.dot(p.astype(BF16), vv[:, hs], preferred_element_type=F32) / l
        o_ref[:, hs] = o
        lse_ref[:, hs] = jnp.broadcast_to(m + jnp.log(l), (qb, hd))


def _dsw_attention_prompt(d, qc, kc, vc, slopes, n_seq, seq_len):
    qb = DSW_QB
    n_tiles = seq_len // DSW_TILE
    nsb = DSW_TILE // qb
    bpc = nsb // d

    def cur(b, t, sb):
        return ((b * n_tiles + t) * nsb + sb, 0)

    def prev(b, t, sb):
        first = (sb & (bpc - 1)) == 0
        back = jnp.where(first, jnp.where(t > 0, nsb - bpc + 1, 0), 1)
        return ((b * n_tiles + t) * nsb + sb - back, 0)

    def nat(b, t, sb):
        return ((b * n_tiles + t) * bpc + (sb & (bpc - 1)), sb // bpc)

    blk = lambda im: pl.BlockSpec((qb, DSW_INNER), im)
    m = n_seq * seq_len
    o, lse = pl.pallas_call(
        functools.partial(_dsw_attn_kernel, d, bpc), grid=(n_seq, n_tiles, nsb),
        in_specs=[pl.BlockSpec(memory_space=pltpu.SMEM), blk(cur), blk(prev), blk(cur), blk(prev), blk(cur)],
        out_specs=[blk(nat), blk(nat)],
        out_shape=[jax.ShapeDtypeStruct((m // d, d * DSW_INNER), F32)] * 2,
        compiler_params=_params(3, 32 * _nbytes((qb, DSW_INNER), F32)),
        name=f"dsw_attn_prompt_d{d}",
    )(slopes, qc, kc, kc, vc, vc)
    return o.reshape(m, DSW_INNER), lse.reshape(m, DSW_INNER)


def _dsw_outln_kernel(o0_ref, o1_ref, o2_ref, l0_ref, l1_ref, l2_ref, w_ref, x_ref, g_ref, b_ref, y_ref):
    l0, l1, l2 = l0_ref[...], l1_ref[...], l2_ref[...]
    m = jnp.maximum(jnp.maximum(l0, l1), l2)
    e0, e1, e2 = jnp.exp(l0 - m), jnp.exp(l1 - m), jnp.exp(l2 - m)
    o = (o0_ref[...] * e0 + o1_ref[...] * e1 + o2_ref[...] * e2) / (e0 + e1 + e2)
    mix = jnp.dot(o.astype(BF16), w_ref[...], preferred_element_type=F32)
    y_ref[...] = _layer_norm(DN_ALPHA * x_ref[...] + mix, g_ref[...], b_ref[...])


def _dsw_out_ln(os_, lses, w, x, g, b, *, tm=512):
    m = x.shape[0]
    tm = min(tm, m)
    part = pl.BlockSpec((tm, DSW_INNER), lambda i: (i, 0))
    row = pl.BlockSpec((tm, D_MODEL), lambda i: (i, 0))
    return pl.pallas_call(
        _dsw_outln_kernel, grid=(m // tm,),
        in_specs=[part] * 6 + [_resident((DSW_INNER, D_MODEL)), row, _resident((1, D_MODEL)), _resident((1, D_MODEL))],
        out_specs=row,
        out_shape=jax.ShapeDtypeStruct((m, D_MODEL), F32),
        compiler_params=_params(1, 24 * _nbytes((tm, DSW_INNER), F32) + 8 * _nbytes((tm, D_MODEL), F32)),
        name="dsw_out_ln",
    )(*os_, *lses, w, x, g.reshape(1, D_MODEL), b.reshape(1, D_MODEL))


def _dsw_sample_kernel(w, d, hps, slope_ref, q_ref, new_ref, buf_ref, out_ref, o_ref, lse_ref):
    hb = pl.program_id(1)
    pos = lax.broadcasted_iota(jnp.int32, (1, w), 1)
    on_dilation = (pos & (d - 1)) == 0
    dist = (w - pos).astype(F32)
    is_last = pos == w - 1
    for hh in range(hps):
        kt = buf_ref[0, 0, hh]
        vt = buf_ref[0, 1, hh]
        qc = q_ref[0, hh] * (DSW_HEAD_DIM ** -0.5)
        k_new = new_ref[0, 0, hh]
        v_new = new_ref[0, 1, hh]
        s = jnp.sum(kt * qc, axis=0, keepdims=True) - slope_ref[hb * hps + hh] * dist
        s = jnp.where(on_dilation, s, -jnp.inf)
        s_new = jnp.sum(qc * k_new, axis=0, keepdims=True)
        m = jnp.maximum(jnp.max(s, axis=-1, keepdims=True), s_new)
        p = jnp.exp(s - m)
        p_new = jnp.exp(s_new - m)
        l = jnp.sum(p, axis=-1, keepdims=True) + p_new
        o_ref[0, hh] = (jnp.sum(vt * p, axis=-1, keepdims=True) + p_new * v_new) / l
        lse_ref[0, hh] = m + jnp.log(l)
        out_ref[0, 0, hh] = jnp.where(is_last, k_new, pltpu.roll(kt, w - 1, axis=1))
        out_ref[0, 1, hh] = jnp.where(is_last, v_new, pltpu.roll(vt, w - 1, axis=1))


def _dsw_sample_group(w, d, q, new_kv, buf, slopes):
    bsz = q.shape[0]
    nh, hd = DSW_HEADS, DSW_HEAD_DIM
    hps = min(nh, max(1, 4096 // w))
    buf_t = jnp.transpose(buf, (0, 2, 3, 4, 1))
    buf_spec = pl.BlockSpec((1, 2, hps, hd, w), lambda b, h: (b, 0, h, 0, 0))
    out_t, o, lse = pl.pallas_call(
        functools.partial(_dsw_sample_kernel, w, d, hps), grid=(bsz, nh // hps),
        in_specs=[pl.BlockSpec(memory_space=pltpu.SMEM),
                  pl.BlockSpec((1, hps, hd, 1), lambda b, h: (b, h, 0, 0)),
                  pl.BlockSpec((1, 2, hps, hd, 1), lambda b, h: (b, 0, h, 0, 0)),
                  buf_spec],
        out_specs=[buf_spec, pl.BlockSpec((1, hps, hd, 1), lambda b, h: (b, h, 0, 0)),
                   pl.BlockSpec((1, hps, 1, 1), lambda b, h: (b, h, 0, 0))],
        out_shape=[jax.ShapeDtypeStruct(buf_t.shape, F32), jax.ShapeDtypeStruct((bsz, nh, hd, 1), F32),
                   jax.ShapeDtypeStruct((bsz, nh, 1, 1), F32)],
        compiler_params=_params(2, 6 * _nbytes((2, hps, hd, w), F32) + (4 << 20)),
        name=f"dsw_sample_d{d}",
    )(slopes, q, new_kv, buf_t)
    return o.reshape(bsz, nh * hd), lse.reshape(bsz, nh), jnp.transpose(out_t, (0, 4, 1, 2, 3))


def kernel(x_prompt, x_sample, cache_sc_conv, cache_sb_k, cache_sb_v, state_gla, cache_dsw_kv0, cache_dsw_kv1,
           cache_dsw_kv2, state_ffn_conv, page_table, ln_g, ln_b, w_sc_in, w_sc_conv, w_sc_out, w_sb_qkv, w_sb_out,
           b_sb, w_gla_in, w_gla_gate_up, b_gla_gate, g_gla_norm, w_gla_out, w_dsw_qkv, w_dsw_out, w_ffn_up,
           w_ffn_conv, w_ffn_down):
    n_seq, seq_len, _ = x_prompt.shape
    bsz = x_sample.shape[0]
    xp = x_prompt.reshape(n_seq * seq_len, D_MODEL)
    xs = x_sample.reshape(bsz, D_MODEL)
    bf = lambda w: w.astype(BF16)
    ffn_p, ffn_s = [], []

    def ffn(layer, xp, xs):
        w_up, w_down = bf(w_ffn_up[layer]), bf(w_ffn_down[layer])
        g, b = ln_g[layer, 1], ln_b[layer, 1]
        xp, st_p = _gconv("ffn", xp, w_up, w_ffn_conv[layer], w_down, g, b, seq_len=seq_len)
        prev = state_ffn_conv[layer]
        xs, u_s = _gconv("ffn", xs, w_up, w_ffn_conv[layer], w_down, g, b, prev=(prev[:, 0], prev[:, 1]))
        ffn_p.append(st_p[:, 6:8])
        ffn_s.append(jnp.stack([prev[:, 1], u_s], axis=1))
        return xp, xs

    w_in, w_out = bf(w_sc_in), bf(w_sc_out)
    xp, sc_st = _gconv("sc", xp, w_in, w_sc_conv, w_out, ln_g[0, 0], ln_b[0, 0], seq_len=seq_len)
    xs, sc_u = _gconv("sc", xs, w_in, w_sc_conv, w_out, ln_g[0, 0], ln_b[0, 0],
                      prev=(cache_sc_conv[:, 0], cache_sc_conv[:, 1]))
    sc_p = sc_st[:, 6:8]
    sc_s = jnp.stack([cache_sc_conv[:, 1], sc_u], axis=1)
    xp, xs = ffn(0, xp, xs)

    w_qkv, w_out = bf(w_sb_qkv), bf(w_sb_out)
    width = SB_HEADS * SB_HEAD_DIM
    scale = SB_HEAD_DIM ** -0.5
    qb, kb, vb = _proj(xp, w_qkv, [(0, width, scale, BF16), (width, width, 1.0, BF16),
                                   (2 * width, width, 1.0, BF16)], name="sb_qkv_prompt")
    o = _sb_attention_prompt(qb, kb, vb, b_sb, n_seq, seq_len)
    kt, vt = _proj_t(xp, bf(w_sb_qkv[:, width:].T), 2, n_seq, seq_len, name="sb_kv_t_prompt")
    sb_k_p = jnp.transpose(kt.reshape(n_seq, SB_HEADS, SB_HEAD_DIM, seq_len), (0, 3, 1, 2))
    sb_v_p = jnp.transpose(vt.reshape(n_seq, SB_HEADS, SB_HEAD_DIM, seq_len), (0, 3, 1, 2))
    xp = _out_ln(o, w_out, xp, ln_g[1, 0], ln_b[1, 0])
    qs, ks, vs = _proj(xs, w_qkv, [(0, width, 1.0, F32), (width, width, 1.0, F32), (2 * width, width, 1.0, F32)],
                       name="sb_qkv_sample")
    o = _sb_attention_sample(qs, cache_sb_k, cache_sb_v, page_table, b_sb)
    sb_k_s = ks.reshape(bsz, 1, SB_HEADS, SB_HEAD_DIM)
    sb_v_s = vs.reshape(bsz, 1, SB_HEADS, SB_HEAD_DIM)
    xs = _out_ln(o, w_out, xs, ln_g[1, 0], ln_b[1, 0])
    xp, xs = ffn(1, xp, xs)

    n_in = w_gla_in.shape[1]
    w_in = bf(jnp.pad(w_gla_in, ((0, 0), (0, 2 * GLA_QK + 2 * GLA_V + 128 - n_in))))
    w_gate = bf(jnp.pad(w_gla_gate_up, ((0, 128 - GLA_RANK), (0, 0))))
    w_out = bf(w_gla_out)
    q, k, v, r, la = _gla_in(xp, w_in, w_gate, b_gla_gate)
    o, gla_p = _gla_prompt(q, k, la, v, r, g_gla_norm, n_seq, seq_len)
    xp = _out_ln(o, w_out, xp, ln_g[2, 0], ln_b[2, 0])
    q, k, v, r, la = _gla_in(xs, w_in, w_gate, b_gla_gate)
    o, gla_s = _gla_sample(q, k, la, v, r, g_gla_norm, state_gla)
    xs = _out_ln(o, w_out, xs, ln_g[2, 0], ln_b[2, 0])
    xp, xs = ffn(2, xp, xs)

    w_qkv, w_out = bf(w_dsw_qkv), bf(w_dsw_out)
    ng = len(DSW_GROUPS)
    n_heads = ng * DSW_HEADS
    slopes = (2.0 ** (-8.0 * jnp.arange(1, n_heads + 1, dtype=F32) / n_heads)).reshape(ng, DSW_HEADS)
    m = n_seq * seq_len
    tm = 512
    os_, lses = [], []
    for g, (_, d) in enumerate(DSW_GROUPS):
        rows_per_class = DSW_TILE // d
        parts = max(1, tm // rows_per_class)
        steps_per_tile = DSW_TILE // tm
        x_map = (lambda i: (i, 0)) if d == 1 else (lambda i: (i // steps_per_tile, i % steps_per_tile))
        base = g * 3 * DSW_INNER
        qc, kc, vc = _proj(xp.reshape(m // d, d * D_MODEL), w_qkv,
                           [(base, DSW_INNER, DSW_HEAD_DIM ** -0.5, BF16), (base + DSW_INNER, DSW_INNER, 1.0, BF16),
                            (base + 2 * DSW_INNER, DSW_INNER, 1.0, BF16)],
                           tm=tm, name=f"dsw_qkv_prompt_d{d}", row_blocks=m // tm, x_map=x_map, parts=parts)
        o, lse = _dsw_attention_prompt(d, qc, kc, vc, slopes[g], n_seq, seq_len)
        os_.append(o)
        lses.append(lse)
    w_max = max(w for w, _ in DSW_GROUPS)
    wt_kv = jnp.concatenate([w_dsw_qkv[:, g * 3 * DSW_INNER + DSW_INNER:(g + 1) * 3 * DSW_INNER] for g in range(ng)],
                            axis=1).T
    kv_tail = _proj_t(xp, bf(wt_kv), ng, n_seq, seq_len, tail_len=w_max, tm=tm, name="dsw_kv_tail")
    dsw_p = [jnp.transpose(kv.reshape(n_seq, 2, DSW_HEADS, DSW_HEAD_DIM, w_max), (0, 4, 1, 2, 3))[:, w_max - w:]
             for kv, (w, _) in zip(kv_tail, DSW_GROUPS)]
    xp = _dsw_out_ln(os_, lses, w_out, xp, ln_g[3, 0], ln_b[3, 0])
    bufs = [cache_dsw_kv0, cache_dsw_kv1, cache_dsw_kv2]
    (qkv,) = _proj(xs, w_qkv, [(0, w_qkv.shape[1], 1.0, F32)], name="dsw_qkv_sample")
    qkv = qkv.reshape(bsz, ng, 3, DSW_HEADS, DSW_HEAD_DIM, 1)
    os_, lses, dsw_s = [], [], []
    for g, (w, d) in enumerate(DSW_GROUPS):
        o, lse, rolled = _dsw_sample_group(w, d, qkv[:, g, 0], qkv[:, g, 1:3], bufs[g], slopes[g])
        os_.append(o)
        lses.append(jnp.repeat(lse, DSW_HEAD_DIM, axis=1))
        dsw_s.append(rolled)
    xs = _dsw_out_ln(os_, lses, w_out, xs, ln_g[3, 0], ln_b[3, 0])
    xp, xs = ffn(3, xp, xs)

    return (xp.reshape(n_seq, seq_len, D_MODEL), xs.reshape(bsz, 1, D_MODEL), sc_p, sc_s, sb_k_p, sb_v_p, sb_k_s,
            sb_v_s, gla_p, gla_s, dsw_p[0], dsw_p[1], dsw_p[2], dsw_s[0], dsw_s[1], dsw_s[2],
            jnp.stack(ffn_p), jnp.stack(ffn_s))
```

```python
import functools

import jax
import jax.numpy as jnp
from jax import lax
from jax.experimental import pallas as pl
from jax.experimental.pallas import tpu as pltpu

F32 = jnp.float32
BF16 = jnp.bfloat16

D_MODEL = 1024
DEPTH = 4
DN_ALPHA = (2.0 * DEPTH) ** 0.25
LN_EPS = 1e-5
PAGE_SIZE = 128

SB_HEADS = 16
SB_HEAD_DIM = 64
SB_BLOCKS_PER_ITER = 4

GLA_HEADS = 4
GLA_DK = 128
GLA_DV = 256
GLA_RANK = 16
GLA_TAU = 16.0
GLA_QK = GLA_HEADS * GLA_DK
GLA_V = GLA_HEADS * GLA_DV
GLA_CHUNK = 128

DSW_GROUPS = ((128, 1), (512, 4), (2048, 16))
DSW_HEADS = 8
DSW_HEAD_DIM = 64
DSW_INNER = DSW_HEADS * DSW_HEAD_DIM
DSW_TILE = 2048
DSW_QB = 128

D_FF = 2816
MXU_COLS = 256
V7X_VMEM_BYTES = 64 * 1024 * 1024
VMEM_CAP = V7X_VMEM_BYTES - 6 * 1024 * 1024


def _vmem_limit(block_bytes):
    return int(min(VMEM_CAP, block_bytes * 5 // 4 + (8 << 20)))


def _params(n_grid, block_bytes):
    return pltpu.CompilerParams(dimension_semantics=("arbitrary",) * n_grid,
                                vmem_limit_bytes=_vmem_limit(block_bytes))


def _resident(shape):
    nd = len(shape)
    return pl.BlockSpec(shape, lambda *_: (0,) * nd, pipeline_mode=pl.Buffered(1))


def _nbytes(shape, dtype):
    n = 1
    for s in shape:
        n *= s
    return n * jnp.dtype(dtype).itemsize


def _layer_norm(y, g, b):
    mu = jnp.mean(y, axis=-1, keepdims=True)
    yc = y - mu
    var = jnp.mean(yc * yc, axis=-1, keepdims=True)
    return yc * lax.rsqrt(var + LN_EPS) * g + b


def _sigmoid(z):
    return 1.0 / (1.0 + jnp.exp(-z))


def _softplus(z):
    return jnp.maximum(z, 0.0) + jnp.log(1.0 + jnp.exp(-jnp.abs(z)))


def _split_hi_lo(x):
    hi = x.astype(BF16)
    lo = (x - hi.astype(F32)).astype(BF16)
    return hi, lo


_NT = (((1,), (1,)), ((), ()))
_TN = (((0,), (0,)), ((), ()))


def _gconv_kernel(kind, step_mode, tm, fdim, tiles_per_seq, *refs):
    cw = MXU_COLS
    if step_mode:
        x_ref, w1_ref, wc_ref, w2_ref, g_ref, b_ref, p0_ref, p1_ref, y_ref, st_ref, act_ref = refs
    else:
        x_ref, w1_ref, wc_ref, w2_ref, g_ref, b_ref, y_ref, st_ref, act_ref, carry_ref, ubuf_ref = refs

        @pl.when(pl.program_id(0) % tiles_per_seq == 0)
        def _():
            carry_ref[...] = jnp.zeros_like(carry_ref)

    x = x_ref[...]
    xb = x.astype(BF16)
    for c in range(fdim // cw):
        lo = c * cw

        def proj(k, lo=lo):
            return jnp.dot(xb, w1_ref[:, k * fdim + lo:k * fdim + lo + cw], preferred_element_type=F32)

        if kind == "ffn":
            u = proj(0)
            other = proj(1)
        else:
            other = proj(0)
            u = proj(1) * proj(2)
        w0 = wc_ref[0:1, lo:lo + cw]
        w1 = wc_ref[1:2, lo:lo + cw]
        w2 = wc_ref[2:3, lo:lo + cw]
        if step_mode:
            z = w0 * p0_ref[:, lo:lo + cw] + w1 * p1_ref[:, lo:lo + cw] + w2 * u
            st_ref[:, lo:lo + cw] = u
        else:
            ubuf_ref[0:8, :] = carry_ref[:, lo:lo + cw]
            ubuf_ref[8:tm + 8, :] = u
            z = w0 * ubuf_ref[6:tm + 6, :] + w1 * ubuf_ref[7:tm + 7, :] + w2 * u
            tail = u[tm - 8:tm, :]
            carry_ref[:, lo:lo + cw] = tail
            st_ref[0, :, lo:lo + cw] = tail
        if kind == "ffn":
            act = z * _sigmoid(z) * other
        else:
            act = other * z
        act_ref[:, lo:lo + cw] = act.astype(BF16)
    mix = jnp.dot(act_ref[...], w2_ref[...], preferred_element_type=F32)
    y_ref[...] = _layer_norm(DN_ALPHA * x + mix, g_ref[...], b_ref[...])


def _gconv(kind, x, w1, wc, w2, g, b, *, seq_len=None, prev=None, tm=512):
    m = x.shape[0]
    fdim = wc.shape[1]
    nw = w1.shape[1]
    step_mode = prev is not None
    g2, b2 = g.reshape(1, D_MODEL), b.reshape(1, D_MODEL)
    common = [_resident((D_MODEL, nw)), _resident((3, fdim)), _resident((fdim, D_MODEL)),
              _resident((1, D_MODEL)), _resident((1, D_MODEL))]
    wbytes = _nbytes((D_MODEL, nw), BF16) + _nbytes((fdim, D_MODEL), BF16)
    if step_mode:
        tm = m
        kern = functools.partial(_gconv_kernel, kind, True, tm, fdim, 1)
        return pl.pallas_call(
            kern, grid=(1,),
            in_specs=[pl.BlockSpec((tm, D_MODEL), lambda i: (0, 0))] + common
            + [pl.BlockSpec((tm, fdim), lambda i: (0, 0))] * 2,
            out_specs=[pl.BlockSpec((tm, D_MODEL), lambda i: (0, 0)), pl.BlockSpec((tm, fdim), lambda i: (0, 0))],
            out_shape=[jax.ShapeDtypeStruct((m, D_MODEL), F32), jax.ShapeDtypeStruct((m, fdim), F32)],
            scratch_shapes=[pltpu.VMEM((tm, fdim), BF16)],
            compiler_params=_params(1, wbytes + 8 * _nbytes((tm, fdim), F32)),
            name=f"gconv_{kind}_step",
        )(x, w1, wc, w2, g2, b2, prev[0], prev[1])
    tiles_per_seq = seq_len // tm
    n_seq = m // seq_len
    kern = functools.partial(_gconv_kernel, kind, False, tm, fdim, tiles_per_seq)
    return pl.pallas_call(
        kern, grid=(m // tm,),
        in_specs=[pl.BlockSpec((tm, D_MODEL), lambda i: (i, 0))] + common,
        out_specs=[pl.BlockSpec((tm, D_MODEL), lambda i: (i, 0)),
                   pl.BlockSpec((1, 8, fdim), lambda i: (i // tiles_per_seq, 0, 0))],
        out_shape=[jax.ShapeDtypeStruct((m, D_MODEL), F32), jax.ShapeDtypeStruct((n_seq, 8, fdim), F32)],
        scratch_shapes=[pltpu.VMEM((tm, fdim), BF16), pltpu.VMEM((8, fdim), F32), pltpu.VMEM((tm + 8, MXU_COLS), F32)],
        compiler_params=_params(1, wbytes + 6 * _nbytes((tm, D_MODEL), F32) + 12 * _nbytes((tm, MXU_COLS), F32)
                                + _nbytes((tm, fdim), BF16)),
        name=f"gconv_{kind}_seq",
    )(x, w1, wc, w2, g2, b2)


def _proj_kernel(n_cols, outs, parts, x_ref, w_ref, *out_refs):
    cw = MXU_COLS
    if parts == 1:
        xb = x_ref[...].astype(BF16)
    else:
        xb = jnp.concatenate([x_ref[:, j * D_MODEL:(j + 1) * D_MODEL].astype(BF16) for j in range(parts)], axis=0)
    for c in range(n_cols // cw):
        lo = c * cw
        val = None
        for (start, width, scale), o_ref in zip(outs, out_refs):
            if start <= lo < start + width:
                if val is None:
                    val = jnp.dot(xb, w_ref[:, lo:lo + cw], preferred_element_type=F32)
                v = val if scale == 1.0 else val * scale
                o_ref[:, lo - start:lo - start + cw] = v.astype(o_ref.dtype)


def _proj(x, w, outs, *, tm=512, name="proj", row_blocks=None, x_map=None, parts=1):
    if x_map is None:
        tm = min(tm, x.shape[0])
        row_blocks, x_map = x.shape[0] // tm, (lambda i: (i, 0))
    m = row_blocks * tm
    n = w.shape[1]
    spec = tuple((s, wd, sc) for s, wd, sc, _ in outs)
    kern = functools.partial(_proj_kernel, n, spec, parts)
    obytes = sum(_nbytes((tm, wd), dt) for _, wd, _, dt in outs)
    return pl.pallas_call(
        kern, grid=(row_blocks,),
        in_specs=[pl.BlockSpec((tm // parts, parts * D_MODEL), x_map), _resident((D_MODEL, n))],
        out_specs=[pl.BlockSpec((tm, wd), lambda i: (i, 0)) for _, wd, _, _ in outs],
        out_shape=[jax.ShapeDtypeStruct((m, wd), dt) for _, wd, _, dt in outs],
        compiler_params=_params(1, _nbytes((D_MODEL, n), BF16) + 2 * obytes + 4 * _nbytes((tm, D_MODEL), F32)),
        name=name,
    )(x, w)


def _proj_t_kernel(n_out, x_ref, wt_ref, *out_refs):
    cw = MXU_COLS
    xb = x_ref[...].astype(BF16)
    width = wt_ref.shape[0] // n_out
    for o, o_ref in enumerate(out_refs):
        for c in range(width // cw):
            rows = wt_ref[o * width + c * cw:o * width + (c + 1) * cw, :]
            o_ref[0, c * cw:(c + 1) * cw, :] = lax.dot_general(rows, xb, _NT, preferred_element_type=F32)


def _proj_t(x, wt, n_out, n_seq, seq_len, *, tail_len=None, tm=512, name="proj_t"):
    tail_len = seq_len if tail_len is None else tail_len
    width = wt.shape[0] // n_out
    per_seq, per_tail = seq_len // tm, tail_len // tm
    spec = pl.BlockSpec((1, width, tm), lambda i: (i // per_tail, 0, i % per_tail))
    return pl.pallas_call(
        functools.partial(_proj_t_kernel, n_out), grid=(n_seq * per_tail,),
        in_specs=[pl.BlockSpec((tm, D_MODEL),
                               lambda i: ((i // per_tail) * per_seq + per_seq - per_tail + i % per_tail, 0)),
                  _resident(wt.shape)],
        out_specs=[spec] * n_out,
        out_shape=[jax.ShapeDtypeStruct((n_seq, width, tail_len), F32)] * n_out,
        compiler_params=_params(1, _nbytes(wt.shape, BF16) + 2 * n_out * _nbytes((width, tm), F32)
                                + 4 * _nbytes((tm, D_MODEL), F32)),
        name=name,
    )(x, wt)


def _outln_kernel(a_ref, w_ref, x_ref, g_ref, b_ref, y_ref):
    mix = jnp.dot(a_ref[...].astype(BF16), w_ref[...], preferred_element_type=F32)
    y_ref[...] = _layer_norm(DN_ALPHA * x_ref[...] + mix, g_ref[...], b_ref[...])


def _out_ln(a, w, x, g, b, *, tm=512):
    m, kin = a.shape
    tm = min(tm, m)
    return pl.pallas_call(
        _outln_kernel, grid=(m // tm,),
        in_specs=[pl.BlockSpec((tm, kin), lambda i: (i, 0)), _resident((kin, D_MODEL)),
                  pl.BlockSpec((tm, D_MODEL), lambda i: (i, 0)), _resident((1, D_MODEL)), _resident((1, D_MODEL))],
        out_specs=pl.BlockSpec((tm, D_MODEL), lambda i: (i, 0)),
        out_shape=jax.ShapeDtypeStruct((m, D_MODEL), F32),
        compiler_params=_params(1, _nbytes((kin, D_MODEL), BF16) + 8 * _nbytes((tm, D_MODEL), F32)),
        name="out_ln",
    )(a, w, x, g.reshape(1, D_MODEL), b.reshape(1, D_MODEL))


def _sb_attn_kernel(t, nkb, bias_ref, q_ref, k_ref, v_ref, o_ref, k2_ref, v2_ref, carry_ref, acc_ref):
    hp = pl.program_id(1)
    i = pl.program_id(2)
    hd = SB_HEAD_DIM
    lw = 2 * hd

    @pl.when(i == 0)
    def _():
        lane = lax.broadcasted_iota(jnp.int32, (t, lw), 1)

        def build(j, c):
            r0 = pl.multiple_of(j * t, t)
            for src, dst in ((k_ref, k2_ref), (v_ref, v2_ref)):
                blk = src[pl.ds(r0, t), :]
                zero = jnp.zeros_like(blk)
                dst[j, 0:t, :] = jnp.where(lane < hd, blk, zero)
                dst[j, t:2 * t, :] = jnp.where(lane >= hd, blk, zero)
            return c

        lax.fori_loop(0, nkb, build, 0)

    lane = lax.broadcasted_iota(jnp.int32, (t, 2 * t), 1)
    row = lax.broadcasted_iota(jnp.int32, (t, 2 * t), 0)
    bias = jnp.where(lane < t, bias_ref[2 * hp], bias_ref[2 * hp + 1])
    causal = (lane & (t - 1)) < row
    ri = lax.broadcasted_iota(jnp.int32, (t, t), 0)
    ci = lax.broadcasted_iota(jnp.int32, (t, t), 1)
    suffix_ones = jnp.where(ri >= ci, 1.0, 0.0).astype(BF16)
    q = q_ref[...]
    carry_ref[...] = jnp.zeros_like(carry_ref)
    acc_ref[...] = jnp.zeros_like(acc_ref)

    def scores(j, masked):
        z = lax.dot_general(q, k2_ref[j], _NT, preferred_element_type=F32) + bias
        sp = _softplus(z)
        if masked:
            sp = jnp.where(causal, sp, 0.0)
        spb = sp.astype(BF16)
        c_loc = [jnp.dot(spb[:, h * t:(h + 1) * t], suffix_ones, preferred_element_type=F32) for h in range(2)]
        return z, c_loc

    def accumulate(j, z, c_loc, masked):
        car = carry_ref[...]
        reps = t // lw
        car_full = jnp.concatenate([car[:, :lw]] * reps + [car[:, lw:]] * reps, axis=1)
        w = jnp.exp(z - jnp.concatenate(c_loc, axis=1) - car_full)
        if masked:
            w = jnp.where(causal, w, 0.0)
        acc_ref[...] += jnp.dot(w.astype(BF16), v2_ref[j], preferred_element_type=F32)
        carry_ref[...] = car + jnp.concatenate([jnp.broadcast_to(c[:, 0:1], (t, lw)) for c in c_loc], axis=1)

    z, c_loc = scores(i, True)
    accumulate(i, z, c_loc, True)

    def blocks(j_hi, n):
        sc = [scores(j_hi - r, False) for r in range(n)]
        for r, (z_r, c_r) in enumerate(sc):
            accumulate(j_hi - r, z_r, c_r, False)

    unroll = SB_BLOCKS_PER_ITER

    def many(s, c):
        blocks(i - 1 - unroll * s, unroll)
        return c

    lax.fori_loop(0, i // unroll, many, 0)
    left = i % unroll
    for n in (2, 1):
        if n < unroll:

            @pl.when((left & n) != 0)
            def _(n=n):
                blocks((left & (2 * n - 1)) - 1, n)

    o_ref[...] = acc_ref[...].astype(o_ref.dtype)


def _sb_attention_prompt(qb, kb, vb, bias, n_seq, seq_len):
    t = 256
    nq = seq_len // t
    lw = 2 * SB_HEAD_DIM
    kern = functools.partial(_sb_attn_kernel, t, nq)
    scratch = 2 * _nbytes((seq_len * 2, lw), BF16) + _nbytes((t, 2 * lw), F32) + _nbytes((t, lw), F32)
    return pl.pallas_call(
        kern, grid=(n_seq, SB_HEADS // 2, nq),
        in_specs=[pl.BlockSpec(memory_space=pltpu.SMEM),
                  pl.BlockSpec((t, lw), lambda b, h, i: (b * nq + i, h)),
                  pl.BlockSpec((seq_len, lw), lambda b, h, i: (b, h)),
                  pl.BlockSpec((seq_len, lw), lambda b, h, i: (b, h))],
        out_specs=pl.BlockSpec((t, lw), lambda b, h, i: (b * nq + i, h)),
        out_shape=jax.ShapeDtypeStruct(qb.shape, BF16),
        scratch_shapes=[pltpu.VMEM((nq, 2 * t, lw), BF16), pltpu.VMEM((nq, 2 * t, lw), BF16),
                        pltpu.VMEM((t, 2 * lw), F32), pltpu.VMEM((t, lw), F32)],
        compiler_params=_params(3, scratch + 4 * _nbytes((seq_len, lw), BF16) + (16 << 20)),
        name="sb_attn_prompt",
    )(bias, qb, kb, vb)


SB_PAGES_PER_STEP = 8


def _sb_sample_kernel(n_steps, pt_ref, q_ref, bias_ref, *refs):
    ppb = SB_PAGES_PER_STEP
    k_refs, v_refs = refs[:ppb], refs[ppb:2 * ppb]
    o_ref, acc_ref, carry_ref = refs[2 * ppb:]
    p = pl.program_id(1)
    nh = SB_HEADS

    @pl.when(p == 0)
    def _():
        acc_ref[...] = jnp.zeros_like(acc_ref)
        carry_ref[...] = jnp.zeros_like(carry_ref)

    qc = q_ref[0] * (SB_HEAD_DIM ** -0.5)
    ri = lax.broadcasted_iota(jnp.int32, (2 * PAGE_SIZE, PAGE_SIZE), 0) & (PAGE_SIZE - 1)
    ci = lax.broadcasted_iota(jnp.int32, (2 * PAGE_SIZE, PAGE_SIZE), 1)
    suffix_ones2 = jnp.where(ri >= ci, 1.0, 0.0).astype(BF16)
    bias = bias_ref[...]

    for k_ref, v_ref in zip(k_refs, v_refs):
        z = jnp.sum(k_ref[0] * qc, axis=1) + bias
        hi, lo = _split_hi_lo(_softplus(z))
        c_loc = jnp.dot(jnp.concatenate([hi, lo], axis=1), suffix_ones2, preferred_element_type=F32)
        car = carry_ref[...]
        w = jnp.exp(z - c_loc - car)
        acc_ref[...] += w[:, None, :] * v_ref[0]
        carry_ref[...] = car + jnp.broadcast_to(c_loc[:, 0:1], (nh, PAGE_SIZE))

    @pl.when(p == n_steps - 1)
    def _():
        o_ref[0] = jnp.sum(acc_ref[...], axis=-1, keepdims=True)


def _sb_attention_sample(q, cache_k, cache_v, page_table, bias):
    bsz, n_pages = page_table.shape
    ppb = SB_PAGES_PER_STEP
    n_steps = n_pages // ppb
    nh, hd = SB_HEADS, SB_HEAD_DIM
    kt = jnp.transpose(cache_k, (0, 2, 3, 1))
    vt = jnp.transpose(cache_v, (0, 2, 3, 1))

    def page_spec(off):
        return pl.BlockSpec((1, nh, hd, PAGE_SIZE), lambda b, p, pt: (pt[b, n_pages - 1 - off - ppb * p], 0, 0, 0))

    pages = [page_spec(off) for off in range(ppb)]
    grid_spec = pltpu.PrefetchScalarGridSpec(
        num_scalar_prefetch=1, grid=(bsz, n_steps),
        in_specs=[pl.BlockSpec((1, nh, hd, 1), lambda b, p, pt: (b, 0, 0, 0)),
                  pl.BlockSpec((nh, 1), lambda b, p, pt: (0, 0))] + pages + pages,
        out_specs=pl.BlockSpec((1, nh, hd, 1), lambda b, p, pt: (b, 0, 0, 0)),
        scratch_shapes=[pltpu.VMEM((nh, hd, PAGE_SIZE), F32), pltpu.VMEM((nh, PAGE_SIZE), F32)])
    out = pl.pallas_call(
        functools.partial(_sb_sample_kernel, n_steps), grid_spec=grid_spec,
        out_shape=jax.ShapeDtypeStruct((bsz, nh, hd, 1), F32),
        compiler_params=_params(2, (4 * ppb + 4) * _nbytes((nh, hd, PAGE_SIZE), F32)),
        name="sb_attn_sample",
    )(page_table, q.reshape(bsz, nh, hd, 1), bias.reshape(nh, 1), *([kt] * ppb), *([vt] * ppb))
    return out.reshape(bsz, nh * hd)


def _gla_in_kernel(x_ref, w_ref, wg_ref, bg_ref, q_ref, k_ref, v_ref, r_ref, la_ref):
    cw = MXU_COLS
    xb = x_ref[...].astype(BF16)

    def cols(lo, n):
        return jnp.dot(xb, w_ref[:, lo:lo + n], preferred_element_type=F32)

    for c in range(GLA_QK // cw):
        q_ref[:, c * cw:(c + 1) * cw] = cols(c * cw, cw) * (GLA_DK ** -0.5)
        k_ref[:, c * cw:(c + 1) * cw] = cols(GLA_QK + c * cw, cw)
    for c in range(GLA_V // cw):
        v_ref[:, c * cw:(c + 1) * cw] = cols(2 * GLA_QK + c * cw, cw)
        r_ref[:, c * cw:(c + 1) * cw] = cols(2 * GLA_QK + GLA_V + c * cw, cw)
    g_low = cols(2 * GLA_QK + 2 * GLA_V, 128).astype(BF16)
    u = jnp.dot(g_low, wg_ref[...], preferred_element_type=F32) + bg_ref[...]
    la_ref[...] = -_softplus(-u) * (1.0 / GLA_TAU)


def _gla_in(x, w_pad, wg_pad, bg, *, tm=512):
    m = x.shape[0]
    tm = min(tm, m)
    n = w_pad.shape[1]
    widths = (GLA_QK, GLA_QK, GLA_V, GLA_V, GLA_QK)
    return pl.pallas_call(
        _gla_in_kernel, grid=(m // tm,),
        in_specs=[pl.BlockSpec((tm, D_MODEL), lambda i: (i, 0)), _resident((D_MODEL, n)),
                  _resident((128, GLA_QK)), _resident((1, GLA_QK))],
        out_specs=[pl.BlockSpec((tm, wd), lambda i: (i, 0)) for wd in widths],
        out_shape=[jax.ShapeDtypeStruct((m, wd), F32) for wd in widths],
        compiler_params=_params(1, _nbytes((D_MODEL, n), BF16) + 12 * _nbytes((tm, D_MODEL), F32)),
        name="gla_in",
    )(x, w_pad, wg_pad, bg.reshape(1, GLA_QK))


def _gla_kernel(t_rows, n_t, q_ref, k_ref, la_ref, v_ref, r_ref, gn_ref, o_ref, s_ref, st_ref):
    t = pl.program_id(2)
    ch = GLA_CHUNK

    @pl.when(t == 0)
    def _():
        st_ref[...] = jnp.zeros_like(st_ref)

    ri = lax.broadcasted_iota(jnp.int32, (ch, ch), 0)
    ci = lax.broadcasted_iota(jnp.int32, (ch, ch), 1)
    tri = ci <= ri
    prefix_ones = jnp.where(tri, 1.0, 0.0).astype(BF16)
    prefix_ones2 = jnp.concatenate([prefix_ones, prefix_ones], axis=1)
    gn = gn_ref[...]
    for c in range(t_rows // ch):
        sl = slice(c * ch, (c + 1) * ch)
        hi, lo = _split_hi_lo(la_ref[sl, :])
        b = jnp.dot(prefix_ones2, jnp.concatenate([hi, lo], axis=0), preferred_element_type=F32)
        mid = b[ch // 2 - 1:ch // 2, :]
        last = b[ch - 1:ch, :]
        qs = q_ref[sl, :] * jnp.exp(b - mid)
        ks = k_ref[sl, :] * jnp.exp(mid - b)
        scores = lax.dot_general(qs.astype(BF16), ks.astype(BF16), _NT, preferred_element_type=F32)
        scores = jnp.where(tri, scores, 0.0)
        vb = v_ref[sl, :].astype(BF16)
        st = st_ref[...]
        qe = (qs * jnp.exp(mid)).astype(BF16)
        o = (jnp.dot(scores.astype(BF16), vb, preferred_element_type=F32)
             + lax.dot_general(qe, st.astype(BF16), _NT, preferred_element_type=F32))
        kd = (ks * jnp.exp(last - mid)).astype(BF16)
        st_ref[...] = jnp.exp(last) * st + lax.dot_general(vb, kd, _TN, preferred_element_type=F32)
        o = o * lax.rsqrt(jnp.mean(o * o, axis=-1, keepdims=True) + LN_EPS) * gn
        r = r_ref[sl, :]
        o_ref[sl, :] = (o * (r * _sigmoid(r))).astype(o_ref.dtype)

    @pl.when(t == n_t - 1)
    def _():
        s_ref[0, 0] = st_ref[...].T


def _gla_prompt(q, k, la, v, r, g_norm, n_seq, seq_len, *, t_rows=512):
    n_t = seq_len // t_rows
    kern = functools.partial(_gla_kernel, t_rows, n_t)
    qk_spec = pl.BlockSpec((t_rows, GLA_DK), lambda b, h, t: (b * n_t + t, h))
    v_spec = pl.BlockSpec((t_rows, GLA_DV), lambda b, h, t: (b * n_t + t, h))
    return pl.pallas_call(
        kern, grid=(n_seq, GLA_HEADS, n_t),
        in_specs=[qk_spec, qk_spec, qk_spec, v_spec, v_spec, pl.BlockSpec((1, GLA_DV), lambda b, h, t: (0, 0))],
        out_specs=[v_spec, pl.BlockSpec((1, 1, GLA_DK, GLA_DV), lambda b, h, t: (b, h, 0, 0))],
        out_shape=[jax.ShapeDtypeStruct((n_seq * seq_len, GLA_V), BF16),
                   jax.ShapeDtypeStruct((n_seq, GLA_HEADS, GLA_DK, GLA_DV), F32)],
        scratch_shapes=[pltpu.VMEM((GLA_DV, GLA_DK), F32)],
        compiler_params=_params(3, 16 * _nbytes((t_rows, GLA_DV), F32)),
        name="gla_prompt",
    )(q, k, la, v, r, g_norm.reshape(1, GLA_DV))


def _gla_sample_kernel(q_ref, k_ref, la_ref, v_ref, r_ref, gn_ref, s0_ref, o_ref, s_ref):
    gn = gn_ref[...]
    for h in range(GLA_HEADS):
        a = jnp.exp(la_ref[0, h])
        vrow = v_ref[0, :, h * GLA_DV:(h + 1) * GLA_DV]
        s_new = a * s0_ref[0, h] + k_ref[0, h] * vrow
        s_ref[0, h] = s_new
        o = jnp.sum(q_ref[0, h] * s_new, axis=0, keepdims=True)
        o = o * lax.rsqrt(jnp.mean(o * o, axis=-1, keepdims=True) + LN_EPS) * gn
        r = r_ref[0, :, h * GLA_DV:(h + 1) * GLA_DV]
        o_ref[0, :, h * GLA_DV:(h + 1) * GLA_DV] = o * (r * _sigmoid(r))


def _gla_sample(q, k, la, v, r, g_norm, s0):
    bsz = q.shape[0]
    col = lambda t: t.reshape(bsz, GLA_HEADS, GLA_DK, 1)
    col_spec = pl.BlockSpec((1, GLA_HEADS, GLA_DK, 1), lambda b: (b, 0, 0, 0))
    row_spec = pl.BlockSpec((1, 1, GLA_V), lambda b: (b, 0, 0))
    st_spec = pl.BlockSpec((1, GLA_HEADS, GLA_DK, GLA_DV), lambda b: (b, 0, 0, 0))
    o, s = pl.pallas_call(
        _gla_sample_kernel, grid=(bsz,),
        in_specs=[col_spec, col_spec, col_spec, row_spec, row_spec,
                  pl.BlockSpec((1, GLA_DV), lambda b: (0, 0)), st_spec],
        out_specs=[row_spec, st_spec],
        out_shape=[jax.ShapeDtypeStruct((bsz, 1, GLA_V), F32), jax.ShapeDtypeStruct(s0.shape, F32)],
        compiler_params=_params(1, 8 * _nbytes((GLA_HEADS, GLA_DK, GLA_DV), F32)),
        name="gla_sample",
    )(col(q), col(k), col(la), v.reshape(bsz, 1, GLA_V), r.reshape(bsz, 1, GLA_V), g_norm.reshape(1, GLA_DV), s0)
    return o.reshape(bsz, GLA_V), s


def _dsw_attn_kernel(d, bpc, slope_ref, q_ref, kp_ref, kc_ref, vp_ref, vc_ref, o_ref, lse_ref):
    t = pl.program_id(1)
    sb = pl.program_id(2)
    qb = DSW_QB
    hd = DSW_HEAD_DIM
    has_prev = jnp.minimum(t + (sb & (bpc - 1)), 1)
    row = lax.broadcasted_iota(jnp.int32, (qb, 2 * qb), 0)
    col = lax.broadcasted_iota(jnp.int32, (qb, 2 * qb), 1)
    steps_back = row + qb - col
    first_col = qb - has_prev * qb
    valid = (steps_back >= 0) & (steps_back <= qb) & (col >= first_col)
    dist = steps_back.astype(F32) * float(d)
    q = q_ref[...]
    kk = jnp.concatenate([kp_ref[...], kc_ref[...]], axis=0)
    vv = jnp.concatenate([vp_ref[...], vc_ref[...]], axis=0)
    lane_kv = lax.broadcasted_iota(jnp.int32, (2 * qb, 2 * hd), 1)
    lane_o = lax.broadcasted_iota(jnp.int32, (qb, 2 * hd), 1)
    for hp in range(DSW_HEADS // 2):
        cs = slice(hp * 2 * hd, (hp + 1) * 2 * hd)
        kblk, vblk = kk[:, cs], vv[:, cs]
        zero = jnp.zeros_like(kblk)
        k2 = jnp.concatenate([jnp.where(lane_kv < hd, kblk, zero), jnp.where(lane_kv >= hd, kblk, zero)], axis=0)
        v2 = jnp.concatenate([jnp.where(lane_kv < hd, vblk, zero), jnp.where(lane_kv >= hd, vblk, zero)], axis=0)
        s2 = lax.dot_general(q[:, cs], k2, _NT, preferred_element_type=F32)
        ps, inv_l, lse = [], [], []
        for hh in range(2):
            s = s2[:, hh * 2 * qb:(hh + 1) * 2 * qb] - slope_ref[2 * hp + hh] * dist
            s = jnp.where(valid, s, -jnp.inf)
            m = jnp.max(s, axis=-1, keepdims=True)
            p = jnp.exp(s - m)
            l = jnp.sum(p, axis=-1, keepdims=True)
            ps.append(p.astype(BF16))
            inv_l.append(1.0 / l)
            lse.append(m + jnp.log(l))
        o2 = jnp.dot(jnp.concatenate(ps, axis=1), v2, preferred_element_type=F32)
        o_ref[:, cs] = o2 * jnp.where(lane_o < hd, inv_l[0], inv_l[1])
        lse_ref[:, cs] = jnp.where(lane_o < hd, lse[0], lse[1])


def _dsw_attention_prompt(d, qc, kc, vc, slopes, n_seq, seq_len):
    qb = DSW_QB
    n_tiles = seq_len // DSW_TILE
    nsb = DSW_TILE // qb
    bpc = nsb // d

    def cur(b, t, sb):
        return ((b * n_tiles + t) * nsb + sb, 0)

    def prev(b, t, sb):
        first = (sb & (bpc - 1)) == 0
        back = jnp.where(first, jnp.where(t > 0, nsb - bpc + 1, 0), 1)
        return ((b * n_tiles + t) * nsb + sb - back, 0)

    def nat(b, t, sb):
        return ((b * n_tiles + t) * bpc + (sb & (bpc - 1)), sb // bpc)

    blk = lambda im: pl.BlockSpec((qb, DSW_INNER), im)
    m = n_seq * seq_len
    o, lse = pl.pallas_call(
        functools.partial(_dsw_attn_kernel, d, bpc), grid=(n_seq, n_tiles, nsb),
        in_specs=[pl.BlockSpec(memory_space=pltpu.SMEM), blk(cur), blk(prev), blk(cur), blk(prev), blk(cur)],
        out_specs=[blk(nat), blk(nat)],
        out_shape=[jax.ShapeDtypeStruct((m // d, d * DSW_INNER), F32)] * 2,
        compiler_params=_params(3, 32 * _nbytes((qb, DSW_INNER), F32)),
        name=f"dsw_attn_prompt_d{d}",
    )(slopes, qc, kc, kc, vc, vc)
    return o.reshape(m, DSW_INNER), lse.reshape(m, DSW_INNER)


def _dsw_outln_kernel(o0_ref, o1_ref, o2_ref, l0_ref, l1_ref, l2_ref, w_ref, x_ref, g_ref, b_ref, y_ref):
    l0, l1, l2 = l0_ref[...], l1_ref[...], l2_ref[...]
    m = jnp.maximum(jnp.maximum(l0, l1), l2)
    e0, e1, e2 = jnp.exp(l0 - m), jnp.exp(l1 - m), jnp.exp(l2 - m)
    o = (o0_ref[...] * e0 + o1_ref[...] * e1 + o2_ref[...] * e2) / (e0 + e1 + e2)
    mix = jnp.dot(o.astype(BF16), w_ref[...], preferred_element_type=F32)
    y_ref[...] = _layer_norm(DN_ALPHA * x_ref[...] + mix, g_ref[...], b_ref[...])


def _dsw_out_ln(os_, lses, w, x, g, b, *, tm=512):
    m = x.shape[0]
    tm = min(tm, m)
    part = pl.BlockSpec((tm, DSW_INNER), lambda i: (i, 0))
    row = pl.BlockSpec((tm, D_MODEL), lambda i: (i, 0))
    return pl.pallas_call(
        _dsw_outln_kernel, grid=(m // tm,),
        in_specs=[part] * 6 + [_resident((DSW_INNER, D_MODEL)), row, _resident((1, D_MODEL)), _resident((1, D_MODEL))],
        out_specs=row,
        out_shape=jax.ShapeDtypeStruct((m, D_MODEL), F32),
        compiler_params=_params(1, 24 * _nbytes((tm, DSW_INNER), F32) + 8 * _nbytes((tm, D_MODEL), F32)),
        name="dsw_out_ln",
    )(*os_, *lses, w, x, g.reshape(1, D_MODEL), b.reshape(1, D_MODEL))


def _dsw_sample_kernel(w, d, hps, slope_ref, q_ref, new_ref, buf_ref, out_ref, o_ref, lse_ref):
    hb = pl.program_id(1)
    pos = lax.broadcasted_iota(jnp.int32, (1, w), 1)
    on_dilation = (pos & (d - 1)) == 0
    dist = (w - pos).astype(F32)
    is_last = pos == w - 1
    for hh in range(hps):
        kt = buf_ref[0, 0, hh]
        vt = buf_ref[0, 1, hh]
        qc = q_ref[0, hh] * (DSW_HEAD_DIM ** -0.5)
        k_new = new_ref[0, 0, hh]
        v_new = new_ref[0, 1, hh]
        s = jnp.sum(kt * qc, axis=0, keepdims=True) - slope_ref[hb * hps + hh] * dist
        s = jnp.where(on_dilation, s, -jnp.inf)
        s_new = jnp.sum(qc * k_new, axis=0, keepdims=True)
        m = jnp.maximum(jnp.max(s, axis=-1, keepdims=True), s_new)
        p = jnp.exp(s - m)
        p_new = jnp.exp(s_new - m)
        l = jnp.sum(p, axis=-1, keepdims=True) + p_new
        o_ref[0, hh] = (jnp.sum(vt * p, axis=-1, keepdims=True) + p_new * v_new) / l
        lse_ref[0, hh] = m + jnp.log(l)
        out_ref[0, 0, hh] = jnp.where(is_last, k_new, pltpu.roll(kt, w - 1, axis=1))
        out_ref[0, 1, hh] = jnp.where(is_last, v_new, pltpu.roll(vt, w - 1, axis=1))


def _dsw_sample_group(w, d, q, new_kv, buf, slopes):
    bsz = q.shape[0]
    nh, hd = DSW_HEADS, DSW_HEAD_DIM
    hps = min(nh, max(1, 4096 // w))
    buf_t = jnp.transpose(buf, (0, 2, 3, 4, 1))
    buf_spec = pl.BlockSpec((1, 2, hps, hd, w), lambda b, h: (b, 0, h, 0, 0))
    out_t, o, lse = pl.pallas_call(
        functools.partial(_dsw_sample_kernel, w, d, hps), grid=(bsz, nh // hps),
        in_specs=[pl.BlockSpec(memory_space=pltpu.SMEM),
                  pl.BlockSpec((1, hps, hd, 1), lambda b, h: (b, h, 0, 0)),
                  pl.BlockSpec((1, 2, hps, hd, 1), lambda b, h: (b, 0, h, 0, 0)),
                  buf_spec],
        out_specs=[buf_spec, pl.BlockSpec((1, hps, hd, 1), lambda b, h: (b, h, 0, 0)),
                   pl.BlockSpec((1, hps, 1, 1), lambda b, h: (b, h, 0, 0))],
        out_shape=[jax.ShapeDtypeStruct(buf_t.shape, F32), jax.ShapeDtypeStruct((bsz, nh, hd, 1), F32),
                   jax.ShapeDtypeStruct((bsz, nh, 1, 1), F32)],
        compiler_params=_params(2, 6 * _nbytes((2, hps, hd, w), F32) + (4 << 20)),
        name=f"dsw_sample_d{d}",
    )(slopes, q, new_kv, buf_t)
    return o.reshape(bsz, nh * hd), lse.reshape(bsz, nh), jnp.transpose(out_t, (0, 4, 1, 2, 3))


def kernel(x_prompt, x_sample, cache_sc_conv, cache_sb_k, cache_sb_v, state_gla, cache_dsw_kv0, cache_dsw_kv1,
           cache_dsw_kv2, state_ffn_conv, page_table, ln_g, ln_b, w_sc_in, w_sc_conv, w_sc_out, w_sb_qkv, w_sb_out,
           b_sb, w_gla_in, w_gla_gate_up, b_gla_gate, g_gla_norm, w_gla_out, w_dsw_qkv, w_dsw_out, w_ffn_up,
           w_ffn_conv, w_ffn_down):
    n_seq, seq_len, _ = x_prompt.shape
    bsz = x_sample.shape[0]
    xp = x_prompt.reshape(n_seq * seq_len, D_MODEL)
    xs = x_sample.reshape(bsz, D_MODEL)
    bf = lambda w: w.astype(BF16)
    ffn_p, ffn_s = [], []

    def ffn(layer, xp, xs):
        w_up, w_down = bf(w_ffn_up[layer]), bf(w_ffn_down[layer])
        g, b = ln_g[layer, 1], ln_b[layer, 1]
        xp, st_p = _gconv("ffn", xp, w_up, w_ffn_conv[layer], w_down, g, b, seq_len=seq_len)
        prev = state_ffn_conv[layer]
        xs, u_s = _gconv("ffn", xs, w_up, w_ffn_conv[layer], w_down, g, b, prev=(prev[:, 0], prev[:, 1]))
        ffn_p.append(st_p[:, 6:8])
        ffn_s.append(jnp.stack([prev[:, 1], u_s], axis=1))
        return xp, xs

    w_in, w_out = bf(w_sc_in), bf(w_sc_out)
    xp, sc_st = _gconv("sc", xp, w_in, w_sc_conv, w_out, ln_g[0, 0], ln_b[0, 0], seq_len=seq_len)
    xs, sc_u = _gconv("sc", xs, w_in, w_sc_conv, w_out, ln_g[0, 0], ln_b[0, 0],
                      prev=(cache_sc_conv[:, 0], cache_sc_conv[:, 1]))
    sc_p = sc_st[:, 6:8]
    sc_s = jnp.stack([cache_sc_conv[:, 1], sc_u], axis=1)
    xp, xs = ffn(0, xp, xs)

    w_qkv, w_out = bf(w_sb_qkv), bf(w_sb_out)
    width = SB_HEADS * SB_HEAD_DIM
    scale = SB_HEAD_DIM ** -0.5
    qb, kb, vb = _proj(xp, w_qkv, [(0, width, scale, BF16), (width, width, 1.0, BF16),
                                   (2 * width, width, 1.0, BF16)], name="sb_qkv_prompt")
    o = _sb_attention_prompt(qb, kb, vb, b_sb, n_seq, seq_len)
    kt, vt = _proj_t(xp, bf(w_sb_qkv[:, width:].T), 2, n_seq, seq_len, name="sb_kv_t_prompt")
    sb_k_p = jnp.transpose(kt.reshape(n_seq, SB_HEADS, SB_HEAD_DIM, seq_len), (0, 3, 1, 2))
    sb_v_p = jnp.transpose(vt.reshape(n_seq, SB_HEADS, SB_HEAD_DIM, seq_len), (0, 3, 1, 2))
    xp = _out_ln(o, w_out, xp, ln_g[1, 0], ln_b[1, 0])
    qs, ks, vs = _proj(xs, w_qkv, [(0, width, 1.0, F32), (width, width, 1.0, F32), (2 * width, width, 1.0, F32)],
                       name="sb_qkv_sample")
    o = _sb_attention_sample(qs, cache_sb_k, cache_sb_v, page_table, b_sb)
    sb_k_s = ks.reshape(bsz, 1, SB_HEADS, SB_HEAD_DIM)
    sb_v_s = vs.reshape(bsz, 1, SB_HEADS, SB_HEAD_DIM)
    xs = _out_ln(o, w_out, xs, ln_g[1, 0], ln_b[1, 0])
    xp, xs = ffn(1, xp, xs)

    n_in = w_gla_in.shape[1]
    w_in = bf(jnp.pad(w_gla_in, ((0, 0), (0, 2 * GLA_QK + 2 * GLA_V + 128 - n_in))))
    w_gate = bf(jnp.pad(w_gla_gate_up, ((0, 128 - GLA_RANK), (0, 0))))
    w_out = bf(w_gla_out)
    q, k, v, r, la = _gla_in(xp, w_in, w_gate, b_gla_gate)
    o, gla_p = _gla_prompt(q, k, la, v, r, g_gla_norm, n_seq, seq_len)
    xp = _out_ln(o, w_out, xp, ln_g[2, 0], ln_b[2, 0])
    q, k, v, r, la = _gla_in(xs, w_in, w_gate, b_gla_gate)
    o, gla_s = _gla_sample(q, k, la, v, r, g_gla_norm, state_gla)
    xs = _out_ln(o, w_out, xs, ln_g[2, 0], ln_b[2, 0])
    xp, xs = ffn(2, xp, xs)

    w_qkv, w_out = bf(w_dsw_qkv), bf(w_dsw_out)
    ng = len(DSW_GROUPS)
    n_heads = ng * DSW_HEADS
    slopes = (2.0 ** (-8.0 * jnp.arange(1, n_heads + 1, dtype=F32) / n_heads)).reshape(ng, DSW_HEADS)
    m = n_seq * seq_len
    tm = 512
    os_, lses = [], []
    for g, (_, d) in enumerate(DSW_GROUPS):
        rows_per_class = DSW_TILE // d
        parts = max(1, tm // rows_per_class)
        steps_per_tile = DSW_TILE // tm
        x_map = (lambda i: (i, 0)) if d == 1 else (lambda i: (i // steps_per_tile, i % steps_per_tile))
        base = g * 3 * DSW_INNER
        qc, kc, vc = _proj(xp.reshape(m // d, d * D_MODEL), w_qkv,
                           [(base, DSW_INNER, DSW_HEAD_DIM ** -0.5, BF16), (base + DSW_INNER, DSW_INNER, 1.0, BF16),
                            (base + 2 * DSW_INNER, DSW_INNER, 1.0, BF16)],
                           tm=tm, name=f"dsw_qkv_prompt_d{d}", row_blocks=m // tm, x_map=x_map, parts=parts)
        o, lse = _dsw_attention_prompt(d, qc, kc, vc, slopes[g], n_seq, seq_len)
        os_.append(o)
        lses.append(lse)
    w_max = max(w for w, _ in DSW_GROUPS)
    wt_kv = jnp.concatenate([w_dsw_qkv[:, g * 3 * DSW_INNER + DSW_INNER:(g + 1) * 3 * DSW_INNER] for g in range(ng)],
                            axis=1).T
    kv_tail = _proj_t(xp, bf(wt_kv), ng, n_seq, seq_len, tail_len=w_max, tm=tm, name="dsw_kv_tail")
    dsw_p = [jnp.transpose(kv.reshape(n_seq, 2, DSW_HEADS, DSW_HEAD_DIM, w_max), (0, 4, 1, 2, 3))[:, w_max - w:]
             for kv, (w, _) in zip(kv_tail, DSW_GROUPS)]
    xp = _dsw_out_ln(os_, lses, w_out, xp, ln_g[3, 0], ln_b[3, 0])
    bufs = [cache_dsw_kv0, cache_dsw_kv1, cache_dsw_kv2]
    (qkv,) = _proj(xs, w_qkv, [(0, w_qkv.shape[1], 1.0, F32)], name="dsw_qkv_sample")
    qkv = qkv.reshape(bsz, ng, 3, DSW_HEADS, DSW_HEAD_DIM, 1)
    os_, lses, dsw_s = [], [], []
    for g, (w, d) in enumerate(DSW_GROUPS):
        o, lse, rolled = _dsw_sample_group(w, d, qkv[:, g, 0], qkv[:, g, 1:3], bufs[g], slopes[g])
        os_.append(o)
        lses.append(jnp.repeat(lse, DSW_HEAD_DIM, axis=1))
        dsw_s.append(rolled)
    xs = _dsw_out_ln(os_, lses, w_out, xs, ln_g[3, 0], ln_b[3, 0])
    xp, xs = ffn(3, xp, xs)

    return (xp.reshape(n_seq, seq_len, D_MODEL), xs.reshape(bsz, 1, D_MODEL), sc_p, sc_s, sb_k_p, sb_v_p, sb_k_s,
            sb_v_s, gla_p, gla_s, dsw_p[0], dsw_p[1], dsw_p[2], dsw_s[0], dsw_s[1], dsw_s[2],
            jnp.stack(ffn_p), jnp.stack(ffn_s))
```
